```python
import jax, jax.numpy as jnp
from jax import lax
import numpy as np

D_MODEL = 1024
BATCH = 8
SEQ = 4096
DEPTH = 4

MIX_WIDTH = D_MODEL
CONV_WIDTH = MIX_WIDTH // 2
CONV_GROUPS = 8
CONV_K = 3
RET_HEADS = 4
RET_WIDTH = MIX_WIDTH - CONV_WIDTH
RET_HEAD_DIM = RET_WIDTH // RET_HEADS
CHUNK = 128
D_FF = 4 * D_MODEL
ROPE_BASE = 10000.0
EPS = 1e-6
N_MOD = 6
IN_SIZES = (CONV_WIDTH, CONV_WIDTH, CONV_WIDTH, RET_WIDTH, RET_WIDTH, RET_WIDTH, RET_WIDTH)
IN_COLS = sum(IN_SIZES)
IN_SPLITS = [int(s) for s in np.cumsum(IN_SIZES)[:-1]]

kernel_name = "hybrid_shortconv_retention_adaln_trunk"


def rms_norm(x, g):
    xf = x.astype(jnp.float32)
    y = xf * lax.rsqrt(jnp.mean(xf * xf, axis=-1, keepdims=True) + EPS)
    return (y * g.astype(jnp.float32)).astype(x.dtype)


def rope_tables(t, d):
    inv_freq = ROPE_BASE ** (-jnp.arange(0, d, 2, dtype=jnp.float32) / d)
    ang = jnp.arange(t, dtype=jnp.float32)[:, None] * inv_freq[None, :]
    return jnp.cos(ang), jnp.sin(ang)


def apply_rope(u, cos, sin):
    half = u.shape[-1] // 2
    u1, u2 = u[..., :half], u[..., half:]
    cs, sn = cos[None, :, None, :], sin[None, :, None, :]
    return jnp.concatenate([u1 * cs - u2 * sn, u1 * sn + u2 * cs], axis=-1)


def short_conv(u, w):
    return lax.conv_general_dilated(
        u, w[:, None, :].astype(u.dtype), window_strides=(1,),
        padding=[(CONV_K - 1, 0)], dimension_numbers=('NWC', 'WIO', 'NWC'),
        feature_group_count=u.shape[-1])


def retention_chunkwise(q, k, v):
    b, t, h, d = q.shape
    n = t // CHUNK
    log_g = jnp.log1p(-jnp.exp2(-5.0 - jnp.arange(h, dtype=jnp.float32)))
    idx = jnp.arange(CHUNK, dtype=jnp.float32)
    diff = idx[:, None] - idx[None, :]
    causal = diff >= 0
    inner_decay = jnp.where(causal[None],
                            jnp.exp(log_g[:, None, None] * jnp.where(causal, diff, 0.0)[None]),
                            0.0)
    xi = jnp.exp(log_g[:, None] * (idx + 1.0)[None])
    zeta = jnp.exp(log_g[:, None] * (CHUNK - 1.0 - idx)[None])
    chunk_decay = jnp.exp(log_g * CHUNK)

    qc = q.reshape(b, n, CHUNK, h, d)
    kc = k.reshape(b, n, CHUNK, h, d)
    vc = v.reshape(b, n, CHUNK, h, d)
    scores = jnp.einsum('bnihd,bnjhd->bnhij', qc, kc) * inner_decay[None, None]
    inner = jnp.einsum('bnhij,bnjhe->bnihe', scores, vc)
    kv = jnp.einsum('bnjhd,bnjhe,hj->nbhde', kc, vc, zeta)

    def step(state, kv_n):
        return state * chunk_decay[None, :, None, None] + kv_n, state

    _, states = lax.scan(step, jnp.zeros((b, h, d, d), jnp.float32), kv)
    cross = jnp.einsum('bnihd,nbhde,hi->bnihe', qc, states, xi)
    return (inner + cross).reshape(b, t, h, d)


def setup_inputs(seed: int = 0) -> dict:
    key = jax.random.key(seed)
    ks = jax.random.split(key, 14)
    f32 = jnp.float32
    x = jax.random.normal(ks[0], (BATCH, SEQ, D_MODEL), f32)
    c = jax.random.normal(ks[1], (BATCH, D_MODEL), f32)
    norm1_g = 1.0 + 0.05 * jax.random.normal(ks[2], (DEPTH, D_MODEL), f32)
    w_in = jax.random.normal(ks[3], (DEPTH, D_MODEL, IN_COLS), f32) * D_MODEL ** -0.5
    conv_w = jax.random.normal(ks[4], (DEPTH, CONV_K, CONV_WIDTH), f32) * CONV_K ** -0.5
    ret_norm_g = 1.0 + 0.05 * jax.random.normal(ks[5], (DEPTH, RET_WIDTH), f32)
    w_out = jax.random.normal(ks[6], (DEPTH, MIX_WIDTH, D_MODEL), f32) * MIX_WIDTH ** -0.5
    norm2_g = 1.0 + 0.05 * jax.random.normal(ks[7], (DEPTH, D_MODEL), f32)
    w_up = jax.random.normal(ks[8], (DEPTH, D_MODEL, D_FF), f32) * D_MODEL ** -0.5
    w_down = jax.random.normal(ks[9], (DEPTH, D_FF, D_MODEL), f32) * D_FF ** -0.5
    w_ada = jax.random.normal(ks[10], (DEPTH, D_MODEL, N_MOD * D_MODEL), f32) * (0.5 * D_MODEL ** -0.5)
    b_ada = 0.02 * jax.random.normal(ks[11], (DEPTH, N_MOD * D_MODEL), f32)
    final_g = 1.0 + 0.05 * jax.random.normal(ks[12], (D_MODEL,), f32)
    return {"x": x, "c": c, "norm1_g": norm1_g, "w_in": w_in, "conv_w": conv_w,
            "ret_norm_g": ret_norm_g, "w_out": w_out, "norm2_g": norm2_g,
            "w_up": w_up, "w_down": w_down, "w_ada": w_ada, "b_ada": b_ada,
            "final_g": final_g}


def reference(x, c, norm1_g, w_in, conv_w, ret_norm_g, w_out, norm2_g,
              w_up, w_down, w_ada, b_ada, final_g):
    b, t, _ = x.shape
    cos, sin = rope_tables(t, RET_HEAD_DIM)
    c_act = jax.nn.silu(c)
    for l in range(DEPTH):
        mod = (c_act @ w_ada[l] + b_ada[l])[:, None, :]
        sh1, sc1, g1, sh2, sc2, g2 = jnp.split(mod, N_MOD, axis=-1)

        h = rms_norm(x, norm1_g[l]) * (1.0 + sc1) + sh1
        proj = h @ w_in[l]
        cb, cc, cu, q, k, v, gr = jnp.split(proj, IN_SPLITS, axis=-1)

        y_conv = cb * short_conv(cc * cu, conv_w[l])

        qh = apply_rope(q.astype(jnp.float32).reshape(b, t, RET_HEADS, RET_HEAD_DIM), cos, sin)
        kh = apply_rope(k.astype(jnp.float32).reshape(b, t, RET_HEADS, RET_HEAD_DIM), cos, sin)
        kh = kh * (RET_HEAD_DIM ** -0.5)
        vh = v.astype(jnp.float32).reshape(b, t, RET_HEADS, RET_HEAD_DIM)
        o = retention_chunkwise(qh, kh, vh)
        mu = jnp.mean(o, axis=-1, keepdims=True)
        var = jnp.mean(jnp.square(o - mu), axis=-1, keepdims=True)
        o = ((o - mu) * lax.rsqrt(var + EPS)).reshape(b, t, RET_WIDTH)
        o = (o * ret_norm_g[l].astype(jnp.float32)).astype(x.dtype)
        y_ret = jax.nn.silu(gr) * o

        mix = jnp.concatenate([y_conv, y_ret], axis=-1) @ w_out[l]
        x = x + g1 * mix

        h2 = rms_norm(x, norm2_g[l]) * (1.0 + sc2) + sh2
        x = x + g2 * (jnp.square(jax.nn.relu(h2 @ w_up[l])) @ w_down[l])

    return rms_norm(x, final_g)
```

```python
import functools

import jax
import jax.numpy as jnp
from jax import lax
from jax.experimental import pallas as pl
from jax.experimental.pallas import tpu as pltpu

D_MODEL = 1024
DEPTH = 4
CONV_WIDTH = 512
CONV_K = 3
RET_HEADS = 4
RET_WIDTH = 512
HEAD_DIM = 128
D_FF = 4 * D_MODEL
ROPE_BASE = 10000.0
EPS = 1e-6
N_MOD = 6
SEG = 512
IN_COLS = 7 * SEG

SEQ_TILE = 512
RET_CHUNK = 256
FF_CHUNK = 1024
HALO = 8
VMEM_LIMIT_BYTES = 56 * 1024 * 1024

F32 = jnp.float32
BF16 = jnp.bfloat16


def _resident(block_shape, index_map):
    return pl.BlockSpec(block_shape, index_map, pipeline_mode=pl.Buffered(1))


def _adaln_kernel(c_ref, w_ref, b_ref, o_ref):
    c = c_ref[...]
    ca = c * jax.nn.sigmoid(c)
    ca_hi = ca.astype(BF16).astype(F32)
    lhs = jnp.concatenate([ca_hi, ca - ca_hi], axis=0).astype(BF16)
    w = w_ref[...]
    w_hi = w.astype(BF16)
    w_lo = (w - w_hi.astype(F32)).astype(BF16)
    acc = (jnp.dot(lhs, w_hi, preferred_element_type=F32)
           + jnp.dot(lhs, w_lo, preferred_element_type=F32))
    nb = c.shape[0]
    o_ref[...] = acc[0:nb] + acc[nb:2 * nb] + b_ref[...]


def _adaln(c, w_ada, b_ada):
    nb = c.shape[0]
    return pl.pallas_call(
        _adaln_kernel,
        grid=(DEPTH, N_MOD),
        in_specs=[
            pl.BlockSpec((nb, D_MODEL), lambda l, j: (0, 0)),
            pl.BlockSpec((None, D_MODEL, D_MODEL), lambda l, j: (l, 0, j)),
            pl.BlockSpec((None, 1, D_MODEL), lambda l, j: (l, 0, j)),
        ],
        out_specs=pl.BlockSpec((None, nb, D_MODEL), lambda l, j: (l, 0, j)),
        out_shape=jax.ShapeDtypeStruct((DEPTH, nb, N_MOD * D_MODEL), F32),
        compiler_params=pltpu.CompilerParams(
            dimension_semantics=("arbitrary", "arbitrary")),
        name="adaln_mod",
    )(c, w_ada, b_ada.reshape(DEPTH, 1, N_MOD * D_MODEL))


def _modulated_rms(x, gain, scale, shift):
    r = lax.rsqrt(jnp.mean(x * x, axis=-1, keepdims=True) + EPS)
    return ((x * r) * gain) * (1.0 + scale) + shift


def _mixer_kernel(x_ref, mod_ref, g_ref, w_in_ref, cw_ref, rg_ref, w_out_ref,
                  cos_ref, sin_ref, dec_ref, xi_ref, zeta_ref, cd_ref,
                  o_ref, ubuf, ybuf, state):
    ts = x_ref.shape[0]

    @pl.when(pl.program_id(1) == 0)
    def _start_of_sequence():
        ubuf[0:HALO, :] = jnp.zeros((HALO, CONV_WIDTH), F32)
        state[...] = jnp.zeros(state.shape, F32)

    x = x_ref[...]
    mod = mod_ref[...]
    shift = mod[:, 0:D_MODEL]
    scale = mod[:, D_MODEL:2 * D_MODEL]
    gate = mod[:, 2 * D_MODEL:3 * D_MODEL]
    hb = _modulated_rms(x, g_ref[...], scale, shift).astype(BF16)

    def proj(i):
        return jnp.dot(hb, w_in_ref[:, i * SEG:(i + 1) * SEG],
                       preferred_element_type=F32)

    u = proj(1) * proj(2)
    ubuf[HALO:HALO + ts, :] = u
    u1 = ubuf[HALO - 1:HALO - 1 + ts, :]
    u2 = ubuf[HALO - 2:HALO - 2 + ts, :]
    cw = cw_ref[...]
    conv = u2 * cw[0:1, :] + u1 * cw[1:2, :] + u * cw[2:3, :]
    ubuf[0:HALO, :] = ubuf[ts:ts + HALO, :]
    ybuf[:, 0:CONV_WIDTH] = (proj(0) * conv).astype(BF16)

    q = proj(3)
    k = proj(4)
    v = proj(5)
    gr = proj(6)
    cos2 = cos_ref[...]
    sin2 = sin_ref[...]
    rc = dec_ref.shape[1]
    for hh in range(RET_HEADS):
        cols = slice(hh * HEAD_DIM, (hh + 1) * HEAD_DIM)
        qh = q[:, cols]
        kh = k[:, cols]
        qh = qh * cos2 + pltpu.roll(qh, HEAD_DIM // 2, 1) * sin2
        kh = kh * cos2 + pltpu.roll(kh, HEAD_DIM // 2, 1) * sin2
        vh = v[:, cols]
        s_h = state[hh]
        for ci in range(ts // rc):
            rows = slice(ci * rc, (ci + 1) * rc)
            qc = qh[rows]
            kc = kh[rows]
            vc = vh[rows]
            scores = lax.dot_general(
                qc.astype(BF16), kc.astype(BF16), (((1,), (1,)), ((), ())),
                preferred_element_type=F32)
            p = (scores * dec_ref[hh]).astype(BF16)
            inner = jnp.dot(p, vc.astype(BF16), preferred_element_type=F32)
            cross = jnp.dot((qc * xi_ref[hh]).astype(BF16), s_h.astype(BF16),
                            preferred_element_type=F32)
            o = inner + cross
            kv = jnp.dot(kc.T.astype(BF16), (vc * zeta_ref[hh]).astype(BF16),
                         preferred_element_type=F32)
            s_h = s_h * cd_ref[hh][0:1, :] + kv
            mu = jnp.mean(o, axis=-1, keepdims=True)
            oc = o - mu
            var = jnp.mean(oc * oc, axis=-1, keepdims=True)
            on = (oc * lax.rsqrt(var + EPS)) * rg_ref[:, cols]
            g = gr[rows, cols]
            y = (g * jax.nn.sigmoid(g)) * on
            ybuf[rows, CONV_WIDTH + hh * HEAD_DIM:
                 CONV_WIDTH + (hh + 1) * HEAD_DIM] = y.astype(BF16)
        state[hh] = s_h

    mix = jnp.dot(ybuf[...], w_out_ref[...], preferred_element_type=F32)
    o_ref[...] = x + gate * mix


def _mixer(layer, x, mod, norm_g, w_in, conv_w, ret_g, w_out, tables):
    nb, t, d = x.shape
    cos2, sin2, dec, xi, zeta, cd = tables
    rc = dec.shape[1]
    const2 = lambda b, s: (0, 0)
    const3 = lambda b, s: (0, 0, 0)
    lay2 = lambda b, s: (layer, 0, 0)
    return pl.pallas_call(
        _mixer_kernel,
        grid=(nb, t // SEQ_TILE),
        in_specs=[
            pl.BlockSpec((None, SEQ_TILE, d), lambda b, s: (b, s, 0)),
            pl.BlockSpec((None, None, 1, N_MOD * d), lambda b, s: (layer, b, 0, 0)),
            _resident((None, 1, d), lay2),
            _resident((None, d, IN_COLS), lay2),
            _resident((None, CONV_K, CONV_WIDTH), lay2),
            _resident((None, 1, RET_WIDTH), lay2),
            _resident((None, d, d), lay2),
            pl.BlockSpec((SEQ_TILE, HEAD_DIM), lambda b, s: (s, 0)),
            pl.BlockSpec((SEQ_TILE, HEAD_DIM), lambda b, s: (s, 0)),
            _resident((RET_HEADS, rc, rc), const3),
            _resident((RET_HEADS, rc, HEAD_DIM), const3),
            _resident((RET_HEADS, rc, HEAD_DIM), const3),
            _resident((RET_HEADS, 8, HEAD_DIM), const3),
        ],
        out_specs=pl.BlockSpec((None, SEQ_TILE, d), lambda b, s: (b, s, 0)),
        out_shape=jax.ShapeDtypeStruct(x.shape, F32),
        scratch_shapes=[
            pltpu.VMEM((SEQ_TILE + HALO, CONV_WIDTH), F32),
            pltpu.VMEM((SEQ_TILE, d), BF16),
            pltpu.VMEM((RET_HEADS, HEAD_DIM, HEAD_DIM), F32),
        ],
        compiler_params=pltpu.CompilerParams(
            dimension_semantics=("arbitrary", "arbitrary"),
            vmem_limit_bytes=VMEM_LIMIT_BYTES),
        name=f"mixer_{layer}",
    )(x, mod, norm_g, w_in, conv_w, ret_g, w_out, cos2, sin2, dec, xi, zeta, cd)


def _mlp_kernel(x_ref, mod_ref, g_ref, w_up_ref, w_dn_ref, fg_ref, o_ref, *, final):
    x = x_ref[...]
    mod = mod_ref[...]
    shift = mod[:, 3 * D_MODEL:4 * D_MODEL]
    scale = mod[:, 4 * D_MODEL:5 * D_MODEL]
    gate = mod[:, 5 * D_MODEL:6 * D_MODEL]
    hb = _modulated_rms(x, g_ref[...], scale, shift).astype(BF16)
    acc = None
    for ci in range(D_FF // FF_CHUNK):
        cols = slice(ci * FF_CHUNK, (ci + 1) * FF_CHUNK)
        up = jnp.dot(hb, w_up_ref[:, cols], preferred_element_type=F32)
        act = jnp.square(jnp.maximum(up, 0.0)).astype(BF16)
        down = jnp.dot(act, w_dn_ref[cols, :], preferred_element_type=F32)
        acc = down if acc is None else acc + down
    y = x + gate * acc
    if final:
        r = lax.rsqrt(jnp.mean(y * y, axis=-1, keepdims=True) + EPS)
        y = (y * r) * fg_ref[...]
    o_ref[...] = y


def _mlp(layer, x, mod, norm_g, w_up, w_down, final_g, final):
    nb, t, d = x.shape
    lay2 = lambda b, s: (layer, 0, 0)
    return pl.pallas_call(
        functools.partial(_mlp_kernel, final=final),
        grid=(nb, t // SEQ_TILE),
        in_specs=[
            pl.BlockSpec((None, SEQ_TILE, d), lambda b, s: (b, s, 0)),
            pl.BlockSpec((None, None, 1, N_MOD * d), lambda b, s: (layer, b, 0, 0)),
            _resident((None, 1, d), lay2),
            _resident((None, d, D_FF), lay2),
            _resident((None, D_FF, d), lay2),
            _resident((1, d), lambda b, s: (0, 0)),
        ],
        out_specs=pl.BlockSpec((None, SEQ_TILE, d), lambda b, s: (b, s, 0)),
        out_shape=jax.ShapeDtypeStruct(x.shape, F32),
        compiler_params=pltpu.CompilerParams(
            dimension_semantics=("arbitrary", "arbitrary"),
            vmem_limit_bytes=VMEM_LIMIT_BYTES),
        name=f"mlp_{layer}",
    )(x, mod, norm_g, w_up, w_down, final_g)


def _tables(t):
    half = HEAD_DIM // 2
    inv_freq = ROPE_BASE ** (-jnp.arange(0, HEAD_DIM, 2, dtype=F32) / HEAD_DIM)
    ang = jnp.arange(t, dtype=F32)[:, None] * inv_freq[None, :]
    cos, sin = jnp.cos(ang), jnp.sin(ang)
    assert cos.shape == (t, half)
    cos2 = jnp.concatenate([cos, cos], axis=-1)
    sin2 = jnp.concatenate([-sin, sin], axis=-1)

    rc = RET_CHUNK
    k_scale = HEAD_DIM ** -0.5
    log_g = jnp.log1p(-jnp.exp2(-5.0 - jnp.arange(RET_HEADS, dtype=F32)))
    idx = jnp.arange(rc, dtype=F32)
    diff = idx[:, None] - idx[None, :]
    causal = diff >= 0
    dec = jnp.where(causal[None],
                    jnp.exp(log_g[:, None, None] * jnp.where(causal, diff, 0.0)[None]),
                    0.0) * k_scale
    xi = jnp.exp(log_g[:, None] * (idx + 1.0)[None])
    zeta = jnp.exp(log_g[:, None] * (rc - 1.0 - idx)[None]) * k_scale
    cd = jnp.exp(log_g * rc)
    lanes = (RET_HEADS, rc, HEAD_DIM)
    return (cos2, sin2, dec,
            jnp.broadcast_to(xi[:, :, None], lanes),
            jnp.broadcast_to(zeta[:, :, None], lanes),
            jnp.broadcast_to(cd[:, None, None], (RET_HEADS, 8, HEAD_DIM)))


def kernel(x, c, norm1_g, w_in, conv_w, ret_norm_g, w_out, norm2_g, w_up, w_down,
           w_ada, b_ada, final_g):
    nb, t, d = x.shape
    assert d == D_MODEL and t % SEQ_TILE == 0 and SEQ_TILE % RET_CHUNK == 0
    assert w_in.shape == (DEPTH, D_MODEL, IN_COLS)
    mod = _adaln(c, w_ada, b_ada).reshape(DEPTH, nb, 1, N_MOD * D_MODEL)
    tables = _tables(t)
    w_in_b = w_in.astype(BF16)
    w_out_b = w_out.astype(BF16)
    w_up_b = w_up.astype(BF16)
    w_down_b = w_down.astype(BF16)
    g1 = norm1_g.reshape(DEPTH, 1, D_MODEL)
    g2 = norm2_g.reshape(DEPTH, 1, D_MODEL)
    rg = ret_norm_g.reshape(DEPTH, 1, RET_WIDTH)
    fg = final_g.reshape(1, D_MODEL)
    for layer in range(DEPTH):
        x = _mixer(layer, x, mod, g1, w_in_b, conv_w, rg, w_out_b, tables)
        x = _mlp(layer, x, mod, g2, w_up_b, w_down_b, fg, final=layer == DEPTH - 1)
    return x
```

```python
import functools

import jax
import jax.numpy as jnp
from jax import lax
from jax.experimental import pallas as pl
from jax.experimental.pallas import tpu as pltpu

D_MODEL = 1024
DEPTH = 4
CONV_WIDTH = 512
CONV_K = 3
RET_HEADS = 4
RET_WIDTH = 512
HEAD_DIM = 128
D_FF = 4 * D_MODEL
ROPE_BASE = 10000.0
EPS = 1e-6
N_MOD = 6
SEG = 512
IN_COLS = 7 * SEG

MIX_TILE = 1024
MLP_TILE = 1024
RET_CHUNK = 256
FF_CHUNK = 1024
HALO = 8
VMEM_LIMIT_BYTES = 56 * 1024 * 1024

F32 = jnp.float32
BF16 = jnp.bfloat16


def _resident(block_shape, index_map):
    return pl.BlockSpec(block_shape, index_map, pipeline_mode=pl.Buffered(1))


def _adaln_kernel(c_ref, w_ref, b_ref, o_ref):
    c = c_ref[...]
    ca = c * jax.nn.sigmoid(c)
    ca_hi = ca.astype(BF16).astype(F32)
    lhs = jnp.concatenate([ca_hi, ca - ca_hi], axis=0).astype(BF16)
    w = w_ref[...]
    w_hi = w.astype(BF16)
    w_lo = (w - w_hi.astype(F32)).astype(BF16)
    acc = (jnp.dot(lhs, w_hi, preferred_element_type=F32)
           + jnp.dot(lhs, w_lo, preferred_element_type=F32))
    nb = c.shape[0]
    o_ref[...] = acc[0:nb] + acc[nb:2 * nb] + b_ref[...]


def _adaln(c, w_ada, b_ada):
    nb = c.shape[0]
    return pl.pallas_call(
        _adaln_kernel,
        grid=(DEPTH, N_MOD),
        in_specs=[
            pl.BlockSpec((nb, D_MODEL), lambda l, j: (0, 0)),
            pl.BlockSpec((None, D_MODEL, D_MODEL), lambda l, j: (l, 0, j)),
            pl.BlockSpec((None, 1, D_MODEL), lambda l, j: (l, 0, j)),
        ],
        out_specs=pl.BlockSpec((None, nb, D_MODEL), lambda l, j: (l, 0, j)),
        out_shape=jax.ShapeDtypeStruct((DEPTH, nb, N_MOD * D_MODEL), F32),
        compiler_params=pltpu.CompilerParams(
            dimension_semantics=("arbitrary", "arbitrary")),
        name="adaln_mod",
    )(c, w_ada, b_ada.reshape(DEPTH, 1, N_MOD * D_MODEL))


def _modulated_rms(x, gain, scale, shift):
    r = lax.rsqrt(jnp.mean(x * x, axis=-1, keepdims=True) + EPS)
    return ((x * r) * gain) * (1.0 + scale) + shift


def _mixer_kernel(x_ref, mod_ref, g_ref, w_in_ref, cw_ref, rg_ref, w_out_ref,
                  cos_ref, sin_ref, dec_ref, xi_ref, zeta_ref, cd_ref,
                  o_ref, ubuf, ybuf, state):
    ts = x_ref.shape[0]

    @pl.when(pl.program_id(1) == 0)
    def _start_of_sequence():
        ubuf[0:HALO, :] = jnp.zeros((HALO, CONV_WIDTH), F32)
        state[...] = jnp.zeros(state.shape, F32)

    x = x_ref[...]
    mod = mod_ref[...]
    shift = mod[:, 0:D_MODEL]
    scale = mod[:, D_MODEL:2 * D_MODEL]
    gate = mod[:, 2 * D_MODEL:3 * D_MODEL]
    hb = _modulated_rms(x, g_ref[...], scale, shift).astype(BF16)

    def proj(i):
        return jnp.dot(hb, w_in_ref[:, i * SEG:(i + 1) * SEG],
                       preferred_element_type=F32)

    u = proj(1) * proj(2)
    ubuf[HALO:HALO + ts, :] = u
    u1 = ubuf[HALO - 1:HALO - 1 + ts, :]
    u2 = ubuf[HALO - 2:HALO - 2 + ts, :]
    cw = cw_ref[...]
    conv = u2 * cw[0:1, :] + u1 * cw[1:2, :] + u * cw[2:3, :]
    ubuf[0:HALO, :] = ubuf[ts:ts + HALO, :]
    ybuf[:, 0:CONV_WIDTH] = (proj(0) * conv).astype(BF16)

    q = proj(3)
    k = proj(4)
    v = proj(5)
    gr = proj(6)
    cos2 = cos_ref[...]
    sin2 = sin_ref[...]
    rc = dec_ref.shape[1]
    for hh in range(RET_HEADS):
        cols = slice(hh * HEAD_DIM, (hh + 1) * HEAD_DIM)
        qh = q[:, cols]
        kh = k[:, cols]
        qh = qh * cos2 + pltpu.roll(qh, HEAD_DIM // 2, 1) * sin2
        kh = kh * cos2 + pltpu.roll(kh, HEAD_DIM // 2, 1) * sin2
        vh = v[:, cols]
        s_h = state[hh]
        for ci in range(ts // rc):
            rows = slice(ci * rc, (ci + 1) * rc)
            qc = qh[rows]
            kc = kh[rows]
            vc = vh[rows]
            scores = lax.dot_general(
                qc.astype(BF16), kc.astype(BF16), (((1,), (1,)), ((), ())),
                preferred_element_type=F32)
            p = (scores * dec_ref[hh]).astype(BF16)
            inner = jnp.dot(p, vc.astype(BF16), preferred_element_type=F32)
            cross = jnp.dot((qc * xi_ref[hh]).astype(BF16), s_h.astype(BF16),
                            preferred_element_type=F32)
            o = inner + cross
            kv = jnp.dot(kc.T.astype(BF16), (vc * zeta_ref[hh]).astype(BF16),
                         preferred_element_type=F32)
            s_h = s_h * cd_ref[hh][0:1, :] + kv
            mu = jnp.mean(o, axis=-1, keepdims=True)
            oc = o - mu
            var = jnp.mean(oc * oc, axis=-1, keepdims=True)
            on = (oc * lax.rsqrt(var + EPS)) * rg_ref[:, cols]
            g = gr[rows, cols]
            y = (g * jax.nn.sigmoid(g)) * on
            ybuf[rows, CONV_WIDTH + hh * HEAD_DIM:
                 CONV_WIDTH + (hh + 1) * HEAD_DIM] = y.astype(BF16)
        state[hh] = s_h

    mix = jnp.dot(ybuf[...], w_out_ref[...], preferred_element_type=F32)
    o_ref[...] = x + gate * mix


def _mixer(layer, x, mod, norm_g, w_in, conv_w, ret_g, w_out, tables):
    nb, t, d = x.shape
    cos2, sin2, dec, xi, zeta, cd = tables
    rc = dec.shape[1]
    ts = MIX_TILE
    const3 = lambda b, s: (0, 0, 0)
    lay2 = lambda b, s: (layer, 0, 0)
    return pl.pallas_call(
        _mixer_kernel,
        grid=(nb, t // ts),
        in_specs=[
            pl.BlockSpec((None, ts, d), lambda b, s: (b, s, 0)),
            pl.BlockSpec((None, None, 1, N_MOD * d), lambda b, s: (layer, b, 0, 0)),
            _resident((None, 1, d), lay2),
            _resident((None, d, IN_COLS), lay2),
            _resident((None, CONV_K, CONV_WIDTH), lay2),
            _resident((None, 1, RET_WIDTH), lay2),
            _resident((None, d, d), lay2),
            pl.BlockSpec((ts, HEAD_DIM), lambda b, s: (s, 0)),
            pl.BlockSpec((ts, HEAD_DIM), lambda b, s: (s, 0)),
            _resident((RET_HEADS, rc, rc), const3),
            _resident((RET_HEADS, rc, HEAD_DIM), const3),
            _resident((RET_HEADS, rc, HEAD_DIM), const3),
            _resident((RET_HEADS, 8, HEAD_DIM), const3),
        ],
        out_specs=pl.BlockSpec((None, ts, d), lambda b, s: (b, s, 0)),
        out_shape=jax.ShapeDtypeStruct(x.shape, F32),
        scratch_shapes=[
            pltpu.VMEM((ts + HALO, CONV_WIDTH), F32),
            pltpu.VMEM((ts, d), BF16),
            pltpu.VMEM((RET_HEADS, HEAD_DIM, HEAD_DIM), F32),
        ],
        compiler_params=pltpu.CompilerParams(
            dimension_semantics=("arbitrary", "arbitrary"),
            vmem_limit_bytes=VMEM_LIMIT_BYTES),
        name=f"mixer_{layer}",
    )(x, mod, norm_g, w_in, conv_w, ret_g, w_out, cos2, sin2, dec, xi, zeta, cd)


def _mlp_kernel(x_ref, mod_ref, g_ref, w_up_ref, w_dn_ref, fg_ref, o_ref, *, final):
    x = x_ref[...]
    mod = mod_ref[...]
    shift = mod[:, 3 * D_MODEL:4 * D_MODEL]
    scale = mod[:, 4 * D_MODEL:5 * D_MODEL]
    gate = mod[:, 5 * D_MODEL:6 * D_MODEL]
    hb = _modulated_rms(x, g_ref[...], scale, shift).astype(BF16)
    acc = None
    for ci in range(D_FF // FF_CHUNK):
        cols = slice(ci * FF_CHUNK, (ci + 1) * FF_CHUNK)
        up = jnp.dot(hb, w_up_ref[:, cols], preferred_element_type=F32)
        act = jnp.square(jnp.maximum(up, 0.0)).astype(BF16)
        down = jnp.dot(act, w_dn_ref[cols, :], preferred_element_type=F32)
        acc = down if acc is None else acc + down
    y = x + gate * acc
    if final:
        r = lax.rsqrt(jnp.mean(y * y, axis=-1, keepdims=True) + EPS)
        y = (y * r) * fg_ref[...]
    o_ref[...] = y


def _mlp(layer, x, mod, norm_g, w_up, w_down, final_g, final):
    nb, t, d = x.shape
    ts = MLP_TILE
    lay2 = lambda b, s: (layer, 0, 0)
    return pl.pallas_call(
        functools.partial(_mlp_kernel, final=final),
        grid=(nb, t // ts),
        in_specs=[
            pl.BlockSpec((None, ts, d), lambda b, s: (b, s, 0)),
            pl.BlockSpec((None, None, 1, N_MOD * d), lambda b, s: (layer, b, 0, 0)),
            _resident((None, 1, d), lay2),
            _resident((None, d, D_FF), lay2),
            _resident((None, D_FF, d), lay2),
            _resident((1, d), lambda b, s: (0, 0)),
        ],
        out_specs=pl.BlockSpec((None, ts, d), lambda b, s: (b, s, 0)),
        out_shape=jax.ShapeDtypeStruct(x.shape, F32),
        compiler_params=pltpu.CompilerParams(
            dimension_semantics=("arbitrary", "arbitrary"),
            vmem_limit_bytes=VMEM_LIMIT_BYTES),
        name=f"mlp_{layer}",
    )(x, mod, norm_g, w_up, w_down, final_g)


def _tables(t):
    half = HEAD_DIM // 2
    inv_freq = ROPE_BASE ** (-jnp.arange(0, HEAD_DIM, 2, dtype=F32) / HEAD_DIM)
    ang = jnp.arange(t, dtype=F32)[:, None] * inv_freq[None, :]
    cos, sin = jnp.cos(ang), jnp.sin(ang)
    assert cos.shape == (t, half)
    cos2 = jnp.concatenate([cos, cos], axis=-1)
    sin2 = jnp.concatenate([-sin, sin], axis=-1)

    rc = RET_CHUNK
    k_scale = HEAD_DIM ** -0.5
    log_g = jnp.log1p(-jnp.exp2(-5.0 - jnp.arange(RET_HEADS, dtype=F32)))
    idx = jnp.arange(rc, dtype=F32)
    diff = idx[:, None] - idx[None, :]
    causal = diff >= 0
    dec = jnp.where(causal[None],
                    jnp.exp(log_g[:, None, None] * jnp.where(causal, diff, 0.0)[None]),
                    0.0) * k_scale
    xi = jnp.exp(log_g[:, None] * (idx + 1.0)[None])
    zeta = jnp.exp(log_g[:, None] * (rc - 1.0 - idx)[None]) * k_scale
    cd = jnp.exp(log_g * rc)
    lanes = (RET_HEADS, rc, HEAD_DIM)
    return (cos2, sin2, dec,
            jnp.broadcast_to(xi[:, :, None], lanes),
            jnp.broadcast_to(zeta[:, :, None], lanes),
            jnp.broadcast_to(cd[:, None, None], (RET_HEADS, 8, HEAD_DIM)))


def kernel(x, c, norm1_g, w_in, conv_w, ret_norm_g, w_out, norm2_g, w_up, w_down,
           w_ada, b_ada, final_g):
    nb, t, d = x.shape
    assert d == D_MODEL and t % MIX_TILE == 0 and t % MLP_TILE == 0
    assert MIX_TILE % RET_CHUNK == 0
    assert w_in.shape == (DEPTH, D_MODEL, IN_COLS)
    mod = _adaln(c, w_ada, b_ada).reshape(DEPTH, nb, 1, N_MOD * D_MODEL)
    tables = _tables(t)
    w_in_b = w_in.astype(BF16)
    w_out_b = w_out.astype(BF16)
    w_up_b = w_up.astype(BF16)
    w_down_b = w_down.astype(BF16)
    g1 = norm1_g.reshape(DEPTH, 1, D_MODEL)
    g2 = norm2_g.reshape(DEPTH, 1, D_MODEL)
    rg = ret_norm_g.reshape(DEPTH, 1, RET_WIDTH)
    fg = final_g.reshape(1, D_MODEL)
    for layer in range(DEPTH):
        x = _mixer(layer, x, mod, g1, w_in_b, conv_w, rg, w_out_b, tables)
        x = _mlp(layer, x, mod, g2, w_up_b, w_down_b, fg, final=layer == DEPTH - 1)
    return x
```

```python
import functools

import jax
import jax.numpy as jnp
from jax import lax
from jax.experimental import pallas as pl
from jax.experimental.pallas import tpu as pltpu

D_MODEL = 1024
DEPTH = 4
CONV_WIDTH = 512
CONV_K = 3
RET_HEADS = 4
RET_WIDTH = 512
HEAD_DIM = 128
D_FF = 4 * D_MODEL
ROPE_BASE = 10000.0
EPS = 1e-6
N_MOD = 6
SEG = 512
IN_COLS = 7 * SEG

SEQ_TILE = 512
RET_CHUNK = 256
FF_CHUNK = 512
HALO = 8
VMEM_LIMIT_BYTES = 60 * 1024 * 1024

F32 = jnp.float32
BF16 = jnp.bfloat16


def _resident(block_shape, index_map):
    return pl.BlockSpec(block_shape, index_map, pipeline_mode=pl.Buffered(1))


def _adaln_kernel(c_ref, w_ref, b_ref, o_ref):
    c = c_ref[...]
    ca = c * jax.nn.sigmoid(c)
    ca_hi = ca.astype(BF16).astype(F32)
    lhs = jnp.concatenate([ca_hi, ca - ca_hi], axis=0).astype(BF16)
    w = w_ref[...]
    w_hi = w.astype(BF16)
    w_lo = (w - w_hi.astype(F32)).astype(BF16)
    acc = (jnp.dot(lhs, w_hi, preferred_element_type=F32)
           + jnp.dot(lhs, w_lo, preferred_element_type=F32))
    nb = c.shape[0]
    o_ref[...] = acc[0:nb] + acc[nb:2 * nb] + b_ref[...]


def _adaln(c, w_ada, b_ada):
    nb = c.shape[0]
    return pl.pallas_call(
        _adaln_kernel,
        grid=(DEPTH, N_MOD),
        in_specs=[
            pl.BlockSpec((nb, D_MODEL), lambda l, j: (0, 0)),
            pl.BlockSpec((None, D_MODEL, D_MODEL), lambda l, j: (l, 0, j)),
            pl.BlockSpec((None, 1, D_MODEL), lambda l, j: (l, 0, j)),
        ],
        out_specs=pl.BlockSpec((None, nb, D_MODEL), lambda l, j: (l, 0, j)),
        out_shape=jax.ShapeDtypeStruct((DEPTH, nb, N_MOD * D_MODEL), F32),
        compiler_params=pltpu.CompilerParams(
            dimension_semantics=("arbitrary", "arbitrary")),
        name="adaln_mod",
    )(c, w_ada, b_ada.reshape(DEPTH, 1, N_MOD * D_MODEL))


def _modulated_rms(x, gain, scale, shift):
    r = lax.rsqrt(jnp.mean(x * x, axis=-1, keepdims=True) + EPS)
    return ((x * r) * gain) * (1.0 + scale) + shift


def _mlp_chain(h_ref, x1_ref, mod, w_up_ref, w_dn_ref, fg_ref, o_ref, final):
    gate = mod[:, 5 * D_MODEL:6 * D_MODEL]
    hb = h_ref[...]
    acc = None
    for ci in range(D_FF // FF_CHUNK):
        cols = slice(ci * FF_CHUNK, (ci + 1) * FF_CHUNK)
        up = jnp.dot(hb, w_up_ref[:, cols], preferred_element_type=F32)
        yield
        act = jnp.square(jnp.maximum(up, 0.0)).astype(BF16)
        down = jnp.dot(act, w_dn_ref[cols, :], preferred_element_type=F32)
        acc = down if acc is None else acc + down
        yield
    y = x1_ref[...] + gate * acc
    if final:
        r = lax.rsqrt(jnp.mean(y * y, axis=-1, keepdims=True) + EPS)
        y = (y * r) * fg_ref[...]
    o_ref[...] = y


def _mixer_chain(x_ref, mod, g1_ref, w_in_ref, cw_ref, rg_ref, w_out_ref, g2_ref,
                 cos_ref, sin_ref, dec_ref, xi_ref, zeta_ref, cd_ref,
                 ubuf, ybuf, state, x1_ref, h_ref, fill):
    ts = x_ref.shape[0]
    x = x_ref[...]
    shift = mod[:, 0:D_MODEL]
    scale = mod[:, D_MODEL:2 * D_MODEL]
    gate = mod[:, 2 * D_MODEL:3 * D_MODEL]
    hb = _modulated_rms(x, g1_ref[...], scale, shift).astype(BF16)

    def proj(i):
        return jnp.dot(hb, w_in_ref[:, i * SEG:(i + 1) * SEG],
                       preferred_element_type=F32)

    fill()
    u = proj(1) * proj(2)
    ubuf[HALO:HALO + ts, :] = u
    u1 = ubuf[HALO - 1:HALO - 1 + ts, :]
    u2 = ubuf[HALO - 2:HALO - 2 + ts, :]
    cw = cw_ref[...]
    conv = u2 * cw[0:1, :] + u1 * cw[1:2, :] + u * cw[2:3, :]
    ubuf[0:HALO, :] = ubuf[ts:ts + HALO, :]
    ybuf[:, 0:CONV_WIDTH] = (proj(0) * conv).astype(BF16)
    fill()

    q = proj(3)
    k = proj(4)
    v = proj(5)
    gr = proj(6)
    fill()
    cos2 = cos_ref[...]
    sin2 = sin_ref[...]
    rc = dec_ref.shape[1]
    for hh in range(RET_HEADS):
        cols = slice(hh * HEAD_DIM, (hh + 1) * HEAD_DIM)
        qh = q[:, cols]
        kh = k[:, cols]
        qh = qh * cos2 + pltpu.roll(qh, HEAD_DIM // 2, 1) * sin2
        kh = kh * cos2 + pltpu.roll(kh, HEAD_DIM // 2, 1) * sin2
        vh = v[:, cols]
        s_h = state[hh]
        for ci in range(ts // rc):
            rows = slice(ci * rc, (ci + 1) * rc)
            qc = qh[rows]
            kc = kh[rows]
            vc = vh[rows]
            scores = lax.dot_general(
                qc.astype(BF16), kc.astype(BF16), (((1,), (1,)), ((), ())),
                preferred_element_type=F32)
            fill()
            p = (scores * dec_ref[hh]).astype(BF16)
            inner = jnp.dot(p, vc.astype(BF16), preferred_element_type=F32)
            cross = jnp.dot((qc * xi_ref[hh]).astype(BF16), s_h.astype(BF16),
                            preferred_element_type=F32)
            o = inner + cross
            kv = jnp.dot(kc.T.astype(BF16), (vc * zeta_ref[hh]).astype(BF16),
                         preferred_element_type=F32)
            s_h = s_h * cd_ref[hh][0:1, :] + kv
            mu = jnp.mean(o, axis=-1, keepdims=True)
            oc = o - mu
            var = jnp.mean(oc * oc, axis=-1, keepdims=True)
            on = (oc * lax.rsqrt(var + EPS)) * rg_ref[:, cols]
            g = gr[rows, cols]
            y = (g * jax.nn.sigmoid(g)) * on
            ybuf[rows, CONV_WIDTH + hh * HEAD_DIM:
                 CONV_WIDTH + (hh + 1) * HEAD_DIM] = y.astype(BF16)
        state[hh] = s_h
        if hh % 2 == 1:
            fill()

    fill()
    mix = jnp.dot(ybuf[...], w_out_ref[...], preferred_element_type=F32)
    x1 = x + gate * mix
    x1_ref[...] = x1
    h_ref[...] = _modulated_rms(
        x1, g2_ref[...], mod[:, 4 * D_MODEL:5 * D_MODEL],
        mod[:, 3 * D_MODEL:4 * D_MODEL]).astype(BF16)


def _layer_kernel(x_ref, mod_cur_ref, mod_prev_ref, g1_ref, w_in_ref, cw_ref, rg_ref,
                  w_out_ref, g2_ref, w_up_ref, w_dn_ref, fg_ref,
                  cos_ref, sin_ref, dec_ref, xi_ref, zeta_ref, cd_ref,
                  o_ref, ubuf, ybuf, state, x1_a, h_a, x1_b, h_b, *,
                  tiles_per_seq, n_tiles, final):
    j = pl.program_id(0)

    @pl.when(j % tiles_per_seq == 0)
    def _start_of_sequence():
        ubuf[0:HALO, :] = jnp.zeros((HALO, CONV_WIDTH), F32)
        state[...] = jnp.zeros(state.shape, F32)

    def step(park, parked):
        if parked is None:
            mlp = iter(())
        else:
            mlp = _mlp_chain(parked[1], parked[0], mod_prev_ref[...], w_up_ref,
                             w_dn_ref, fg_ref, o_ref, final)
        if park is not None:
            _mixer_chain(x_ref, mod_cur_ref[...], g1_ref, w_in_ref, cw_ref, rg_ref,
                         w_out_ref, g2_ref, cos_ref, sin_ref, dec_ref, xi_ref,
                         zeta_ref, cd_ref, ubuf, ybuf, state, park[0], park[1],
                         fill=lambda: next(mlp, None))
        for _ in mlp:
            pass

    buf_a, buf_b = (x1_a, h_a), (x1_b, h_b)
    last_parked = buf_b if n_tiles % 2 == 0 else buf_a

    @pl.when(j == 0)
    def _first():
        step(buf_a, None)

    @pl.when((j % 2 == 0) & (j > 0) & (j < n_tiles))
    def _even():
        step(buf_a, buf_b)

    @pl.when((j % 2 == 1) & (j < n_tiles))
    def _odd():
        step(buf_b, buf_a)

    @pl.when(j == n_tiles)
    def _last():
        step(None, last_parked)


def _layer(layer, x, mod, g1, w_in, conv_w, rg, w_out, g2, w_up, w_down, fg, tables,
           final):
    nb, t, d = x.shape
    cos2, sin2, dec, xi, zeta, cd = tables
    rc = dec.shape[1]
    ts = SEQ_TILE
    ns = t // ts
    n_tiles = nb * ns

    def cur(j):
        return jnp.minimum(j, n_tiles - 1)

    def prev(j):
        return jnp.maximum(j - 1, 0)

    const3 = lambda j: (0, 0, 0)
    lay2 = lambda j: (layer, 0, 0)
    return pl.pallas_call(
        functools.partial(_layer_kernel, tiles_per_seq=ns, n_tiles=n_tiles, final=final),
        grid=(n_tiles + 1,),
        in_specs=[
            pl.BlockSpec((None, ts, d), lambda j: (cur(j) // ns, cur(j) % ns, 0)),
            pl.BlockSpec((None, None, 1, N_MOD * d),
                         lambda j: (layer, cur(j) // ns, 0, 0)),
            pl.BlockSpec((None, None, 1, N_MOD * d),
                         lambda j: (layer, prev(j) // ns, 0, 0)),
            _resident((None, 1, d), lay2),
            _resident((None, d, IN_COLS), lay2),
            _resident((None, CONV_K, CONV_WIDTH), lay2),
            _resident((None, 1, RET_WIDTH), lay2),
            _resident((None, d, d), lay2),
            _resident((None, 1, d), lay2),
            _resident((None, d, D_FF), lay2),
            _resident((None, D_FF, d), lay2),
            _resident((1, d), lambda j: (0, 0)),
            pl.BlockSpec((ts, HEAD_DIM), lambda j: (cur(j) % ns, 0)),
            pl.BlockSpec((ts, HEAD_DIM), lambda j: (cur(j) % ns, 0)),
            _resident((RET_HEADS, rc, rc), const3),
            _resident((RET_HEADS, rc, HEAD_DIM), const3),
            _resident((RET_HEADS, rc, HEAD_DIM), const3),
            _resident((RET_HEADS, 8, HEAD_DIM), const3),
        ],
        out_specs=pl.BlockSpec((None, ts, d), lambda j: (prev(j) // ns, prev(j) % ns, 0)),
        out_shape=jax.ShapeDtypeStruct(x.shape, F32),
        scratch_shapes=[
            pltpu.VMEM((ts + HALO, CONV_WIDTH), F32),
            pltpu.VMEM((ts, d), BF16),
            pltpu.VMEM((RET_HEADS, HEAD_DIM, HEAD_DIM), F32),
            pltpu.VMEM((ts, d), F32),
            pltpu.VMEM((ts, d), BF16),
            pltpu.VMEM((ts, d), F32),
            pltpu.VMEM((ts, d), BF16),
        ],
        compiler_params=pltpu.CompilerParams(
            dimension_semantics=("arbitrary",),
            vmem_limit_bytes=VMEM_LIMIT_BYTES),
        name=f"layer_{layer}",
    )(x, mod, mod, g1, w_in, conv_w, rg, w_out, g2, w_up, w_down, fg,
      cos2, sin2, dec, xi, zeta, cd)


def _tables(t):
    half = HEAD_DIM // 2
    inv_freq = ROPE_BASE ** (-jnp.arange(0, HEAD_DIM, 2, dtype=F32) / HEAD_DIM)
    ang = jnp.arange(t, dtype=F32)[:, None] * inv_freq[None, :]
    cos, sin = jnp.cos(ang), jnp.sin(ang)
    assert cos.shape == (t, half)
    cos2 = jnp.concatenate([cos, cos], axis=-1)
    sin2 = jnp.concatenate([-sin, sin], axis=-1)

    rc = RET_CHUNK
    k_scale = HEAD_DIM ** -0.5
    log_g = jnp.log1p(-jnp.exp2(-5.0 - jnp.arange(RET_HEADS, dtype=F32)))
    idx = jnp.arange(rc, dtype=F32)
    diff = idx[:, None] - idx[None, :]
    causal = diff >= 0
    dec = jnp.where(causal[None],
                    jnp.exp(log_g[:, None, None] * jnp.where(causal, diff, 0.0)[None]),
                    0.0) * k_scale
    xi = jnp.exp(log_g[:, None] * (idx + 1.0)[None])
    zeta = jnp.exp(log_g[:, None] * (rc - 1.0 - idx)[None]) * k_scale
    cd = jnp.exp(log_g * rc)
    lanes = (RET_HEADS, rc, HEAD_DIM)
    return (cos2, sin2, dec,
            jnp.broadcast_to(xi[:, :, None], lanes),
            jnp.broadcast_to(zeta[:, :, None], lanes),
            jnp.broadcast_to(cd[:, None, None], (RET_HEADS, 8, HEAD_DIM)))


def kernel(x, c, norm1_g, w_in, conv_w, ret_norm_g, w_out, norm2_g, w_up, w_down,
           w_ada, b_ada, final_g):
    nb, t, d = x.shape
    assert d == D_MODEL and t % SEQ_TILE == 0 and SEQ_TILE % RET_CHUNK == 0
    assert w_in.shape == (DEPTH, D_MODEL, IN_COLS)
    mod = _adaln(c, w_ada, b_ada).reshape(DEPTH, nb, 1, N_MOD * D_MODEL)
    tables = _tables(t)
    w_in_b = w_in.astype(BF16)
    w_out_b = w_out.astype(BF16)
    w_up_b = w_up.astype(BF16)
    w_down_b = w_down.astype(BF16)
    g1 = norm1_g.reshape(DEPTH, 1, D_MODEL)
    g2 = norm2_g.reshape(DEPTH, 1, D_MODEL)
    rg = ret_norm_g.reshape(DEPTH, 1, RET_WIDTH)
    fg = final_g.reshape(1, D_MODEL)
    for layer in range(DEPTH):
        x = _layer(layer, x, mod, g1, w_in_b, conv_w, rg, w_out_b, g2, w_up_b,
                   w_down_b, fg, tables, final=layer == DEPTH - 1)
    return x
```

```python
import functools

import jax
import jax.numpy as jnp
from jax import lax
from jax.experimental import pallas as pl
from jax.experimental.pallas import tpu as pltpu

D_MODEL = 1024
DEPTH = 4
CONV_WIDTH = 512
CONV_K = 3
RET_HEADS = 4
RET_WIDTH = 512
HEAD_DIM = 128
D_FF = 4 * D_MODEL
ROPE_BASE = 10000.0
EPS = 1e-6
N_MOD = 6
SEG = 512
IN_COLS = 7 * SEG

SEQ_TILE = 512
RET_CHUNK = 256
FF_CHUNK = 512
HALO = 8
VMEM_LIMIT_BYTES = 60 * 1024 * 1024

F32 = jnp.float32
BF16 = jnp.bfloat16


def _resident(block_shape, index_map):
    return pl.BlockSpec(block_shape, index_map, pipeline_mode=pl.Buffered(1))


def _adaln_kernel(c_ref, w_ref, b_ref, o_ref):
    c = c_ref[...]
    ca = c * jax.nn.sigmoid(c)
    ca_hi = ca.astype(BF16).astype(F32)
    lhs = jnp.concatenate([ca_hi, ca - ca_hi], axis=0).astype(BF16)
    w = w_ref[...]
    w_hi = w.astype(BF16)
    w_lo = (w - w_hi.astype(F32)).astype(BF16)
    acc = (jnp.dot(lhs, w_hi, preferred_element_type=F32)
           + jnp.dot(lhs, w_lo, preferred_element_type=F32))
    nb = c.shape[0]
    o_ref[...] = acc[0:nb] + acc[nb:2 * nb] + b_ref[...]


def _adaln(c, w_ada, b_ada):
    nb = c.shape[0]
    return pl.pallas_call(
        _adaln_kernel,
        grid=(DEPTH, N_MOD),
        in_specs=[
            pl.BlockSpec((nb, D_MODEL), lambda l, j: (0, 0)),
            pl.BlockSpec((None, D_MODEL, D_MODEL), lambda l, j: (l, 0, j)),
            pl.BlockSpec((None, 1, D_MODEL), lambda l, j: (l, 0, j)),
        ],
        out_specs=pl.BlockSpec((None, nb, D_MODEL), lambda l, j: (l, 0, j)),
        out_shape=jax.ShapeDtypeStruct((DEPTH, nb, N_MOD * D_MODEL), F32),
        compiler_params=pltpu.CompilerParams(
            dimension_semantics=("arbitrary", "arbitrary")),
        name="adaln_mod",
    )(c, w_ada, b_ada.reshape(DEPTH, 1, N_MOD * D_MODEL))


def _modulated_rms(x, gain, scale, shift):
    r = lax.rsqrt(jnp.mean(x * x, axis=-1, keepdims=True) + EPS)
    return ((x * r) * gain) * (1.0 + scale) + shift


def _mlp_chain(h_ref, x1_ref, mod, w_up_ref, w_dn_ref, fg_ref, o_ref, final):
    gate = mod[:, 5 * D_MODEL:6 * D_MODEL]
    hb = h_ref[...]
    acc = None
    for ci in range(D_FF // FF_CHUNK):
        cols = slice(ci * FF_CHUNK, (ci + 1) * FF_CHUNK)
        up = jnp.dot(hb, w_up_ref[:, cols], preferred_element_type=F32)
        yield
        act = jnp.square(jnp.maximum(up, 0.0)).astype(BF16)
        down = jnp.dot(act, w_dn_ref[cols, :], preferred_element_type=F32)
        acc = down if acc is None else acc + down
        yield
    y = x1_ref[...] + gate * acc
    if final:
        r = lax.rsqrt(jnp.mean(y * y, axis=-1, keepdims=True) + EPS)
        y = (y * r) * fg_ref[...]
    o_ref[...] = y


def _mixer_chain(x_ref, mod, g1_ref, w_in_ref, cw_ref, rg_ref, w_out_ref, g2_ref,
                 cos_ref, sin_ref, dec_ref, xi_ref, zeta_ref, cd_ref,
                 ubuf, ybuf, state, x1_ref, h_ref, fill):
    ts = x_ref.shape[0]
    x = x_ref[...]
    shift = mod[:, 0:D_MODEL]
    scale = mod[:, D_MODEL:2 * D_MODEL]
    gate = mod[:, 2 * D_MODEL:3 * D_MODEL]
    hb = _modulated_rms(x, g1_ref[...], scale, shift).astype(BF16)

    def proj(i):
        return jnp.dot(hb, w_in_ref[:, i * SEG:(i + 1) * SEG],
                       preferred_element_type=F32)

    fill()
    u = proj(1) * proj(2)
    ubuf[HALO:HALO + ts, :] = u
    u1 = ubuf[HALO - 1:HALO - 1 + ts, :]
    u2 = ubuf[HALO - 2:HALO - 2 + ts, :]
    cw = cw_ref[...]
    conv = u2 * cw[0:1, :] + u1 * cw[1:2, :] + u * cw[2:3, :]
    ubuf[0:HALO, :] = ubuf[ts:ts + HALO, :]
    ybuf[:, 0:CONV_WIDTH] = (proj(0) * conv).astype(BF16)
    fill()

    q = proj(3)
    k = proj(4)
    v = proj(5)
    gr = proj(6)
    fill()
    cos2 = cos_ref[...]
    sin2 = sin_ref[...]
    rc = dec_ref.shape[1]
    for hh in range(RET_HEADS):
        cols = slice(hh * HEAD_DIM, (hh + 1) * HEAD_DIM)
        qh = q[:, cols]
        kh = k[:, cols]
        qh = qh * cos2 + pltpu.roll(qh, HEAD_DIM // 2, 1) * sin2
        kh = kh * cos2 + pltpu.roll(kh, HEAD_DIM // 2, 1) * sin2
        vh = v[:, cols]
        s_h = state[hh]
        for ci in range(ts // rc):
            rows = slice(ci * rc, (ci + 1) * rc)
            qc = qh[rows]
            kc = kh[rows]
            vc = vh[rows]
            scores = lax.dot_general(
                qc.astype(BF16), kc.astype(BF16), (((1,), (1,)), ((), ())),
                preferred_element_type=F32)
            fill()
            p = (scores * dec_ref[hh]).astype(BF16)
            inner = jnp.dot(p, vc.astype(BF16), preferred_element_type=F32)
            cross = jnp.dot((qc * xi_ref[hh]).astype(BF16), s_h.astype(BF16),
                            preferred_element_type=F32)
            o = inner + cross
            kv = jnp.dot(kc.T.astype(BF16), (vc * zeta_ref[hh]).astype(BF16),
                         preferred_element_type=F32)
            s_h = s_h * cd_ref[hh][0:1, :] + kv
            mu = jnp.mean(o, axis=-1, keepdims=True)
            oc = o - mu
            var = jnp.mean(oc * oc, axis=-1, keepdims=True)
            on = (oc * lax.rsqrt(var + EPS)) * rg_ref[:, cols]
            g = gr[rows, cols]
            y = (g * jax.nn.sigmoid(g)) * on
            ybuf[rows, CONV_WIDTH + hh * HEAD_DIM:
                 CONV_WIDTH + (hh + 1) * HEAD_DIM] = y.astype(BF16)
        state[hh] = s_h
        if hh % 2 == 1:
            fill()

    fill()
    mix = jnp.dot(ybuf[...], w_out_ref[...], preferred_element_type=F32)
    x1 = x + gate * mix
    x1_ref[...] = x1
    h_ref[...] = _modulated_rms(
        x1, g2_ref[...], mod[:, 4 * D_MODEL:5 * D_MODEL],
        mod[:, 3 * D_MODEL:4 * D_MODEL]).astype(BF16)


def _layer_kernel(x_ref, mod_cur_ref, mod_prev_ref, g1_ref, w_in_ref, cw_ref, rg_ref,
                  w_out_ref, g2_ref, w_up_ref, w_dn_ref, fg_ref,
                  cos_ref, sin_ref, dec_ref, xi_ref, zeta_ref, cd_ref,
                  o_ref, ubuf, ybuf, state, x1_park, h_park, x1_prev, h_prev, *,
                  tiles_per_seq, final):
    j = pl.program_id(0)

    @pl.when(j % tiles_per_seq == 0)
    def _start_of_sequence():
        ubuf[0:HALO, :] = jnp.zeros((HALO, CONV_WIDTH), F32)
        state[...] = jnp.zeros(state.shape, F32)

    def mixer(fill):
        _mixer_chain(x_ref, mod_cur_ref[...], g1_ref, w_in_ref, cw_ref, rg_ref,
                     w_out_ref, g2_ref, cos_ref, sin_ref, dec_ref, xi_ref, zeta_ref,
                     cd_ref, ubuf, ybuf, state, x1_park, h_park, fill)

    @pl.when(j == 0)
    def _first():
        mixer(fill=lambda: None)

    @pl.when(j > 0)
    def _steady():
        x1_prev[...] = x1_park[...]
        h_prev[...] = h_park[...]
        mlp = _mlp_chain(h_prev, x1_prev, mod_prev_ref[...], w_up_ref, w_dn_ref,
                         fg_ref, o_ref, final)
        mixer(fill=lambda: next(mlp, None))
        for _ in mlp:
            pass


def _layer(layer, x, mod, g1, w_in, conv_w, rg, w_out, g2, w_up, w_down, fg, tables,
           final):
    nb, t, d = x.shape
    cos2, sin2, dec, xi, zeta, cd = tables
    rc = dec.shape[1]
    ts = SEQ_TILE
    ns = t // ts
    n_tiles = nb * ns

    def cur(j):
        return jnp.minimum(j, n_tiles - 1)

    def prev(j):
        return jnp.maximum(j - 1, 0)

    const3 = lambda j: (0, 0, 0)
    lay2 = lambda j: (layer, 0, 0)
    return pl.pallas_call(
        functools.partial(_layer_kernel, tiles_per_seq=ns, final=final),
        grid=(n_tiles + 1,),
        in_specs=[
            pl.BlockSpec((None, ts, d), lambda j: (cur(j) // ns, cur(j) % ns, 0)),
            pl.BlockSpec((None, None, 1, N_MOD * d),
                         lambda j: (layer, cur(j) // ns, 0, 0)),
            pl.BlockSpec((None, None, 1, N_MOD * d),
                         lambda j: (layer, prev(j) // ns, 0, 0)),
            _resident((None, 1, d), lay2),
            _resident((None, d, IN_COLS), lay2),
            _resident((None, CONV_K, CONV_WIDTH), lay2),
            _resident((None, 1, RET_WIDTH), lay2),
            _resident((None, d, d), lay2),
            _resident((None, 1, d), lay2),
            _resident((None, d, D_FF), lay2),
            _resident((None, D_FF, d), lay2),
            _resident((1, d), lambda j: (0, 0)),
            pl.BlockSpec((ts, HEAD_DIM), lambda j: (cur(j) % ns, 0)),
            pl.BlockSpec((ts, HEAD_DIM), lambda j: (cur(j) % ns, 0)),
            _resident((RET_HEADS, rc, rc), const3),
            _resident((RET_HEADS, rc, HEAD_DIM), const3),
            _resident((RET_HEADS, rc, HEAD_DIM), const3),
            _resident((RET_HEADS, 8, HEAD_DIM), const3),
        ],
        out_specs=pl.BlockSpec((None, ts, d), lambda j: (prev(j) // ns, prev(j) % ns, 0)),
        out_shape=jax.ShapeDtypeStruct(x.shape, F32),
        scratch_shapes=[
            pltpu.VMEM((ts + HALO, CONV_WIDTH), F32),
            pltpu.VMEM((ts, d), BF16),
            pltpu.VMEM((RET_HEADS, HEAD_DIM, HEAD_DIM), F32),
            pltpu.VMEM((ts, d), F32),
            pltpu.VMEM((ts, d), BF16),
            pltpu.VMEM((ts, d), F32),
            pltpu.VMEM((ts, d), BF16),
        ],
        compiler_params=pltpu.CompilerParams(
            dimension_semantics=("arbitrary",),
            vmem_limit_bytes=VMEM_LIMIT_BYTES),
        name=f"layer_{layer}",
    )(x, mod, mod, g1, w_in, conv_w, rg, w_out, g2, w_up, w_down, fg,
      cos2, sin2, dec, xi, zeta, cd)


def _tables(t):
    half = HEAD_DIM // 2
    inv_freq = ROPE_BASE ** (-jnp.arange(0, HEAD_DIM, 2, dtype=F32) / HEAD_DIM)
    ang = jnp.arange(t, dtype=F32)[:, None] * inv_freq[None, :]
    cos, sin = jnp.cos(ang), jnp.sin(ang)
    assert cos.shape == (t, half)
    cos2 = jnp.concatenate([cos, cos], axis=-1)
    sin2 = jnp.concatenate([-sin, sin], axis=-1)

    rc = RET_CHUNK
    k_scale = HEAD_DIM ** -0.5
    log_g = jnp.log1p(-jnp.exp2(-5.0 - jnp.arange(RET_HEADS, dtype=F32)))
    idx = jnp.arange(rc, dtype=F32)
    diff = idx[:, None] - idx[None, :]
    causal = diff >= 0
    dec = jnp.where(causal[None],
                    jnp.exp(log_g[:, None, None] * jnp.where(causal, diff, 0.0)[None]),
                    0.0) * k_scale
    xi = jnp.exp(log_g[:, None] * (idx + 1.0)[None])
    zeta = jnp.exp(log_g[:, None] * (rc - 1.0 - idx)[None]) * k_scale
    cd = jnp.exp(log_g * rc)
    lanes = (RET_HEADS, rc, HEAD_DIM)
    return (cos2, sin2, dec,
            jnp.broadcast_to(xi[:, :, None], lanes),
            jnp.broadcast_to(zeta[:, :, None], lanes),
            jnp.broadcast_to(cd[:, None, None], (RET_HEADS, 8, HEAD_DIM)))


def kernel(x, c, norm1_g, w_in, conv_w, ret_norm_g, w_out, norm2_g, w_up, w_down,
           w_ada, b_ada, final_g):
    nb, t, d = x.shape
    assert d == D_MODEL and t % SEQ_TILE == 0 and SEQ_TILE % RET_CHUNK == 0
    assert w_in.shape == (DEPTH, D_MODEL, IN_COLS)
    mod = _adaln(c, w_ada, b_ada).reshape(DEPTH, nb, 1, N_MOD * D_MODEL)
    tables = _tables(t)
    w_in_b = w_in.astype(BF16)
    w_out_b = w_out.astype(BF16)
    w_up_b = w_up.astype(BF16)
    w_down_b = w_down.astype(BF16)
    g1 = norm1_g.reshape(DEPTH, 1, D_MODEL)
    g2 = norm2_g.reshape(DEPTH, 1, D_MODEL)
    rg = ret_norm_g.reshape(DEPTH, 1, RET_WIDTH)
    fg = final_g.reshape(1, D_MODEL)
    for layer in range(DEPTH):
        x = _layer(layer, x, mod, g1, w_in_b, conv_w, rg, w_out_b, g2, w_up_b,
                   w_down_b, fg, tables, final=layer == DEPTH - 1)
    return x
```

```python
import functools

import jax
import jax.numpy as jnp
from jax import lax
from jax.experimental import pallas as pl
from jax.experimental.pallas import tpu as pltpu

D_MODEL = 1024
DEPTH = 4
CONV_WIDTH = 512
CONV_K = 3
RET_HEADS = 4
RET_WIDTH = 512
HEAD_DIM = 128
D_FF = 4 * D_MODEL
ROPE_BASE = 10000.0
EPS = 1e-6
N_MOD = 6
SEG = 512
IN_COLS = 7 * SEG

MIX_TILE = 1024
MLP_TILE = 1024
RET_CHUNK = 256
FF_CHUNK = 1024
HALO = 8
LOOKAHEAD = 2
VMEM_LIMIT_BYTES = 56 * 1024 * 1024

F32 = jnp.float32
BF16 = jnp.bfloat16
BF16_SUBLANES = 16


def _resident(block_shape, index_map):
    return pl.BlockSpec(block_shape, index_map, pipeline_mode=pl.Buffered(1))


def _adaln_kernel(c_ref, w_ref, b_ref, o_ref):
    c = c_ref[...]
    ca = c * jax.nn.sigmoid(c)
    ca_hi = ca.astype(BF16).astype(F32)
    lhs = jnp.concatenate([ca_hi, ca - ca_hi], axis=0).astype(BF16)
    w = w_ref[...]
    w_hi = w.astype(BF16)
    w_lo = (w - w_hi.astype(F32)).astype(BF16)
    acc = (jnp.dot(lhs, w_hi, preferred_element_type=F32)
           + jnp.dot(lhs, w_lo, preferred_element_type=F32))
    nb = c.shape[0]
    o_ref[...] = acc[0:nb] + acc[nb:2 * nb] + b_ref[...]


def _adaln(c, w_ada, b_ada):
    nb = c.shape[0]
    return pl.pallas_call(
        _adaln_kernel,
        grid=(DEPTH, N_MOD),
        in_specs=[
            pl.BlockSpec((nb, D_MODEL), lambda l, j: (0, 0)),
            pl.BlockSpec((None, D_MODEL, D_MODEL), lambda l, j: (l, 0, j)),
            pl.BlockSpec((None, 1, D_MODEL), lambda l, j: (l, 0, j)),
        ],
        out_specs=pl.BlockSpec((None, nb, D_MODEL), lambda l, j: (l, 0, j)),
        out_shape=jax.ShapeDtypeStruct((DEPTH, nb, N_MOD * D_MODEL), F32),
        compiler_params=pltpu.CompilerParams(
            dimension_semantics=("arbitrary", "arbitrary")),
        name="adaln_mod",
    )(c, w_ada, b_ada.reshape(DEPTH, 1, N_MOD * D_MODEL))


def _modulated_rms(x, gain, scale, shift):
    r = lax.rsqrt(jnp.mean(x * x, axis=-1, keepdims=True) + EPS)
    return ((x * r) * gain) * (1.0 + scale) + shift


def _mixer_kernel(x_ref, mod_ref, g_ref, w_in_ref, cw_ref, rg_ref, w_out_ref,
                  cos_ref, sin_ref, dec_ref, xi_ref, zeta_ref, cd_ref,
                  w_up_f32_ref, w_dn_f32_ref,
                  o_ref, w_up_bf16_ref, w_dn_bf16_ref, ubuf, ybuf, state):
    ts = x_ref.shape[0]

    w_up_bf16_ref[...] = w_up_f32_ref[...].astype(BF16)
    w_dn_bf16_ref[...] = w_dn_f32_ref[...].astype(BF16)

    @pl.when(pl.program_id(1) == 0)
    def _start_of_sequence():
        ubuf[0:HALO, :] = jnp.zeros((HALO, CONV_WIDTH), F32)
        state[...] = jnp.zeros(state.shape, F32)

    x = x_ref[...]
    mod = mod_ref[...]
    shift = mod[:, 0:D_MODEL]
    scale = mod[:, D_MODEL:2 * D_MODEL]
    gate = mod[:, 2 * D_MODEL:3 * D_MODEL]
    hb = _modulated_rms(x, g_ref[...], scale, shift).astype(BF16)

    def proj(i):
        return jnp.dot(hb, w_in_ref[:, i * SEG:(i + 1) * SEG],
                       preferred_element_type=F32)

    q = proj(3)
    k = proj(4)
    v = proj(5)

    u = proj(1) * proj(2)
    ubuf[HALO:HALO + ts, :] = u
    u1 = ubuf[HALO - 1:HALO - 1 + ts, :]
    u2 = ubuf[HALO - 2:HALO - 2 + ts, :]
    cw = cw_ref[...]
    conv = u2 * cw[0:1, :] + u1 * cw[1:2, :] + u * cw[2:3, :]
    ubuf[0:HALO, :] = ubuf[ts:ts + HALO, :]

    cos2 = cos_ref[...]
    sin2 = sin_ref[...]
    rc = dec_ref.shape[1]
    n_chunks = ts // rc
    pairs = [(hh, ci) for hh in range(RET_HEADS) for ci in range(n_chunks)]
    head_cols = [slice(hh * HEAD_DIM, (hh + 1) * HEAD_DIM) for hh in range(RET_HEADS)]
    chunk_rows = [slice(ci * rc, (ci + 1) * rc) for ci in range(n_chunks)]
    qh, kh = [], []
    for cols in head_cols:
        qc, kc = q[:, cols], k[:, cols]
        qh.append(qc * cos2 + pltpu.roll(qc, HEAD_DIM // 2, 1) * sin2)
        kh.append(kc * cos2 + pltpu.roll(kc, HEAD_DIM // 2, 1) * sin2)

    gr = proj(6)

    scores, kv = {}, {}
    s_now = [state[hh] for hh in range(RET_HEADS)]
    for step in range(len(pairs) + LOOKAHEAD):
        if step < len(pairs):
            hh, ci = pairs[step]
            rows = chunk_rows[ci]
            scores[hh, ci] = lax.dot_general(
                qh[hh][rows].astype(BF16), kh[hh][rows].astype(BF16),
                (((1,), (1,)), ((), ())), preferred_element_type=F32)
            vz = v[rows, head_cols[hh]] * zeta_ref[hh]
            kv[hh, ci] = jnp.dot(kh[hh][rows].T.astype(BF16), vz.astype(BF16),
                                 preferred_element_type=F32)
        if step >= LOOKAHEAD:
            hh, ci = pairs[step - LOOKAHEAD]
            rows, cols = chunk_rows[ci], head_cols[hh]
            p = (scores.pop((hh, ci)) * dec_ref[hh]).astype(BF16)
            inner = jnp.dot(p, v[rows, cols].astype(BF16),
                            preferred_element_type=F32)
            cross = jnp.dot((qh[hh][rows] * xi_ref[hh]).astype(BF16),
                            s_now[hh].astype(BF16), preferred_element_type=F32)
            s_now[hh] = s_now[hh] * cd_ref[hh][0:1, :] + kv.pop((hh, ci))
            o = inner + cross
            mu = jnp.mean(o, axis=-1, keepdims=True)
            oc = o - mu
            var = jnp.mean(oc * oc, axis=-1, keepdims=True)
            on = (oc * lax.rsqrt(var + EPS)) * rg_ref[:, cols]
            g = gr[rows, cols]
            y = (g * jax.nn.sigmoid(g)) * on
            ybuf[rows, CONV_WIDTH + hh * HEAD_DIM:
                 CONV_WIDTH + (hh + 1) * HEAD_DIM] = y.astype(BF16)
    for hh in range(RET_HEADS):
        state[hh] = s_now[hh]

    cb = proj(0)
    ybuf[:, 0:CONV_WIDTH] = (cb * conv).astype(BF16)

    mix = (jnp.dot(ybuf[:, CONV_WIDTH:], w_out_ref[CONV_WIDTH:, :],
                   preferred_element_type=F32)
           + jnp.dot(ybuf[:, 0:CONV_WIDTH], w_out_ref[0:CONV_WIDTH, :],
                     preferred_element_type=F32))
    o_ref[...] = x + gate * mix


def _cast_rows(n_rows, n_steps):
    rows = n_rows // n_steps
    assert rows * n_steps == n_rows and rows % BF16_SUBLANES == 0
    return rows


def _mixer(layer, x, mod, norm_g, w_in, conv_w, ret_g, w_out, w_up, w_down, tables):
    nb, t, d = x.shape
    cos2, sin2, dec, xi, zeta, cd = tables
    rc = dec.shape[1]
    ts = MIX_TILE
    ns = t // ts
    up_rows, dn_rows = _cast_rows(D_MODEL, nb * ns), _cast_rows(D_FF, nb * ns)
    const2 = lambda b, s: (0, 0)
    const3 = lambda b, s: (0, 0, 0)
    lay2 = lambda b, s: (layer, 0, 0)
    return pl.pallas_call(
        _mixer_kernel,
        grid=(nb, ns),
        in_specs=[
            pl.BlockSpec((None, ts, d), lambda b, s: (b, s, 0)),
            pl.BlockSpec((None, None, 1, N_MOD * d), lambda b, s: (layer, b, 0, 0)),
            _resident((None, 1, d), lay2),
            _resident((d, IN_COLS), const2),
            _resident((None, CONV_K, CONV_WIDTH), lay2),
            _resident((None, 1, RET_WIDTH), lay2),
            _resident((d, d), const2),
            pl.BlockSpec((ts, HEAD_DIM), lambda b, s: (s, 0)),
            pl.BlockSpec((ts, HEAD_DIM), lambda b, s: (s, 0)),
            _resident((RET_HEADS, rc, rc), const3),
            _resident((RET_HEADS, rc, HEAD_DIM), const3),
            _resident((RET_HEADS, rc, HEAD_DIM), const3),
            _resident((RET_HEADS, 8, HEAD_DIM), const3),
            pl.BlockSpec((None, up_rows, D_FF), lambda b, s: (layer, b * ns + s, 0)),
            pl.BlockSpec((None, dn_rows, d), lambda b, s: (layer, b * ns + s, 0)),
        ],
        out_specs=[
            pl.BlockSpec((None, ts, d), lambda b, s: (b, s, 0)),
            pl.BlockSpec((up_rows, D_FF), lambda b, s: (b * ns + s, 0)),
            pl.BlockSpec((dn_rows, d), lambda b, s: (b * ns + s, 0)),
        ],
        out_shape=[
            jax.ShapeDtypeStruct(x.shape, F32),
            jax.ShapeDtypeStruct((D_MODEL, D_FF), BF16),
            jax.ShapeDtypeStruct((D_FF, D_MODEL), BF16),
        ],
        scratch_shapes=[
            pltpu.VMEM((ts + HALO, CONV_WIDTH), F32),
            pltpu.VMEM((ts, d), BF16),
            pltpu.VMEM((RET_HEADS, HEAD_DIM, HEAD_DIM), F32),
        ],
        compiler_params=pltpu.CompilerParams(
            dimension_semantics=("arbitrary", "arbitrary"),
            vmem_limit_bytes=VMEM_LIMIT_BYTES),
        name=f"mixer_{layer}",
    )(x, mod, norm_g, w_in, conv_w, ret_g, w_out, cos2, sin2, dec, xi, zeta, cd,
      w_up, w_down)


def _mlp_kernel(x_ref, mod_ref, g_ref, w_up_ref, w_dn_ref, fg_ref, *rest,
                final, cast_next):
    if cast_next:
        w_in_f32_ref, w_out_f32_ref, o_ref, w_in_bf16_ref, w_out_bf16_ref = rest
        w_in_bf16_ref[...] = w_in_f32_ref[...].astype(BF16)
        w_out_bf16_ref[...] = w_out_f32_ref[...].astype(BF16)
    else:
        o_ref, = rest

    x = x_ref[...]
    mod = mod_ref[...]
    shift = mod[:, 3 * D_MODEL:4 * D_MODEL]
    scale = mod[:, 4 * D_MODEL:5 * D_MODEL]
    gate = mod[:, 5 * D_MODEL:6 * D_MODEL]
    hb = _modulated_rms(x, g_ref[...], scale, shift).astype(BF16)
    acc = None
    for ci in range(D_FF // FF_CHUNK):
        cols = slice(ci * FF_CHUNK, (ci + 1) * FF_CHUNK)
        up = jnp.dot(hb, w_up_ref[:, cols], preferred_element_type=F32)
        act = jnp.square(jnp.maximum(up, 0.0)).astype(BF16)
        down = jnp.dot(act, w_dn_ref[cols, :], preferred_element_type=F32)
        acc = down if acc is None else acc + down
    y = x + gate * acc
    if final:
        r = lax.rsqrt(jnp.mean(y * y, axis=-1, keepdims=True) + EPS)
        y = (y * r) * fg_ref[...]
    o_ref[...] = y


def _mlp(layer, x, mod, norm_g, w_up, w_down, final_g, w_in, w_out):
    nb, t, d = x.shape
    ts = MLP_TILE
    ns = t // ts
    final = layer == DEPTH - 1
    const2 = lambda b, s: (0, 0)
    in_specs = [
        pl.BlockSpec((None, ts, d), lambda b, s: (b, s, 0)),
        pl.BlockSpec((None, None, 1, N_MOD * d), lambda b, s: (layer, b, 0, 0)),
        _resident((None, 1, d), lambda b, s: (layer, 0, 0)),
        _resident((d, D_FF), const2),
        _resident((D_FF, d), const2),
        _resident((1, d), const2),
    ]
    operands = [x, mod, norm_g, w_up, w_down, final_g]
    out_specs = [pl.BlockSpec((None, ts, d), lambda b, s: (b, s, 0))]
    out_shape = [jax.ShapeDtypeStruct(x.shape, F32)]
    if not final:
        rows = _cast_rows(D_MODEL, nb * ns)
        step_rows = lambda b, s: (b * ns + s, 0)
        next_rows = lambda b, s: (layer + 1, b * ns + s, 0)
        in_specs += [pl.BlockSpec((None, rows, IN_COLS), next_rows),
                     pl.BlockSpec((None, rows, d), next_rows)]
        operands += [w_in, w_out]
        out_specs += [pl.BlockSpec((rows, IN_COLS), step_rows),
                      pl.BlockSpec((rows, d), step_rows)]
        out_shape += [jax.ShapeDtypeStruct((D_MODEL, IN_COLS), BF16),
                      jax.ShapeDtypeStruct((D_MODEL, D_MODEL), BF16)]
    return pl.pallas_call(
        functools.partial(_mlp_kernel, final=final, cast_next=not final),
        grid=(nb, ns),
        in_specs=in_specs,
        out_specs=out_specs,
        out_shape=out_shape,
        compiler_params=pltpu.CompilerParams(
            dimension_semantics=("arbitrary", "arbitrary"),
            vmem_limit_bytes=VMEM_LIMIT_BYTES),
        name=f"mlp_{layer}",
    )(*operands)


def _tables(t):
    half = HEAD_DIM // 2
    inv_freq = ROPE_BASE ** (-jnp.arange(0, HEAD_DIM, 2, dtype=F32) / HEAD_DIM)
    ang = jnp.arange(t, dtype=F32)[:, None] * inv_freq[None, :]
    cos, sin = jnp.cos(ang), jnp.sin(ang)
    assert cos.shape == (t, half)
    cos2 = jnp.concatenate([cos, cos], axis=-1)
    sin2 = jnp.concatenate([-sin, sin], axis=-1)

    rc = RET_CHUNK
    k_scale = HEAD_DIM ** -0.5
    log_g = jnp.log1p(-jnp.exp2(-5.0 - jnp.arange(RET_HEADS, dtype=F32)))
    idx = jnp.arange(rc, dtype=F32)
    diff = idx[:, None] - idx[None, :]
    causal = diff >= 0
    dec = jnp.where(causal[None],
                    jnp.exp(log_g[:, None, None] * jnp.where(causal, diff, 0.0)[None]),
                    0.0) * k_scale
    xi = jnp.exp(log_g[:, None] * (idx + 1.0)[None])
    zeta = jnp.exp(log_g[:, None] * (rc - 1.0 - idx)[None]) * k_scale
    cd = jnp.exp(log_g * rc)
    lanes = (RET_HEADS, rc, HEAD_DIM)
    return (cos2, sin2, dec,
            jnp.broadcast_to(xi[:, :, None], lanes),
            jnp.broadcast_to(zeta[:, :, None], lanes),
            jnp.broadcast_to(cd[:, None, None], (RET_HEADS, 8, HEAD_DIM)))


def kernel(x, c, norm1_g, w_in, conv_w, ret_norm_g, w_out, norm2_g, w_up, w_down,
           w_ada, b_ada, final_g):
    nb, t, d = x.shape
    assert d == D_MODEL and t % MIX_TILE == 0 and t % MLP_TILE == 0
    assert MIX_TILE % RET_CHUNK == 0
    assert w_in.shape == (DEPTH, D_MODEL, IN_COLS)
    mod = _adaln(c, w_ada, b_ada).reshape(DEPTH, nb, 1, N_MOD * D_MODEL)
    tables = _tables(t)
    g1 = norm1_g.reshape(DEPTH, 1, D_MODEL)
    g2 = norm2_g.reshape(DEPTH, 1, D_MODEL)
    rg = ret_norm_g.reshape(DEPTH, 1, RET_WIDTH)
    fg = final_g.reshape(1, D_MODEL)
    w_in_b = w_in[0].astype(BF16)
    w_out_b = w_out[0].astype(BF16)
    for layer in range(DEPTH):
        x, w_up_b, w_down_b = _mixer(layer, x, mod, g1, w_in_b, conv_w, rg, w_out_b,
                                     w_up, w_down, tables)
        outs = _mlp(layer, x, mod, g2, w_up_b, w_down_b, fg, w_in, w_out)
        if layer < DEPTH - 1:
            x, w_in_b, w_out_b = outs
        else:
            x, = outs
    return x
```

```python
import functools

import jax
import jax.numpy as jnp
import numpy as np
from jax import lax
from jax.experimental import pallas as pl
from jax.experimental.pallas import tpu as pltpu

D_MODEL = 1024
DEPTH = 4
CONV_WIDTH = 512
CONV_K = 3
RET_HEADS = 4
RET_WIDTH = 512
HEAD_DIM = 128
D_FF = 4 * D_MODEL
ROPE_BASE = 10000.0
EPS = 1e-6
N_MOD = 6
SEG = 512
IN_COLS = 7 * SEG

MIX_TILE = 1024
MLP_TILE = 1024
RET_CHUNK = 256
FF_CHUNK = 1024
ADA_COLS = 1536
HALO = 8
LOOKAHEAD = 2
VMEM_LIMIT_BYTES = 56 * 1024 * 1024

F32 = jnp.float32
BF16 = jnp.bfloat16
BF16_SUBLANES = 16


def _resident(block_shape, index_map):
    return pl.BlockSpec(block_shape, index_map, pipeline_mode=pl.Buffered(1))


def _adaln_kernel(c_ref, w_ref, b_ref, w_in_f32_ref, w_out_f32_ref,
                  o_ref, w_in_bf16_ref, w_out_bf16_ref):
    w_in_bf16_ref[...] = w_in_f32_ref[...].astype(BF16)
    w_out_bf16_ref[...] = w_out_f32_ref[...].astype(BF16)

    c = c_ref[...]
    ca = c * jax.nn.sigmoid(c)
    ca_hi = ca.astype(BF16).astype(F32)
    lhs = jnp.concatenate([ca_hi, ca - ca_hi], axis=0).astype(BF16)
    w = w_ref[...]
    w_hi = w.astype(BF16)
    w_lo = (w - w_hi.astype(F32)).astype(BF16)
    acc = (jnp.dot(lhs, w_hi, preferred_element_type=F32)
           + jnp.dot(lhs, w_lo, preferred_element_type=F32))
    nb = c.shape[0]
    o_ref[...] = acc[0:nb] + acc[nb:2 * nb] + b_ref[...]


def _adaln(c, w_ada, b_ada, w_in, w_out):
    nb = c.shape[0]
    n_col = N_MOD * D_MODEL // ADA_COLS
    rows = _cast_rows(D_MODEL, DEPTH * n_col)
    step_rows = lambda l, j: (l * n_col + j, 0)
    first_rows = lambda l, j: (0, l * n_col + j, 0)
    return pl.pallas_call(
        _adaln_kernel,
        grid=(DEPTH, n_col),
        in_specs=[
            pl.BlockSpec((nb, D_MODEL), lambda l, j: (0, 0)),
            pl.BlockSpec((None, D_MODEL, ADA_COLS), lambda l, j: (l, 0, j)),
            pl.BlockSpec((None, 1, ADA_COLS), lambda l, j: (l, 0, j)),
            pl.BlockSpec((None, rows, IN_COLS), first_rows),
            pl.BlockSpec((None, rows, D_MODEL), first_rows),
        ],
        out_specs=[
            pl.BlockSpec((None, nb, ADA_COLS), lambda l, j: (l, 0, j)),
            pl.BlockSpec((rows, IN_COLS), step_rows),
            pl.BlockSpec((rows, D_MODEL), step_rows),
        ],
        out_shape=[
            jax.ShapeDtypeStruct((DEPTH, nb, N_MOD * D_MODEL), F32),
            jax.ShapeDtypeStruct((D_MODEL, IN_COLS), BF16),
            jax.ShapeDtypeStruct((D_MODEL, D_MODEL), BF16),
        ],
        compiler_params=pltpu.CompilerParams(
            dimension_semantics=("arbitrary", "arbitrary"),
            vmem_limit_bytes=VMEM_LIMIT_BYTES),
        name="adaln_mod",
    )(c, w_ada, b_ada.reshape(DEPTH, 1, N_MOD * D_MODEL), w_in, w_out)


def _modulated_rms(x, gain, scale, shift):
    r = lax.rsqrt(jnp.mean(x * x, axis=-1, keepdims=True) + EPS)
    return ((x * r) * gain) * (1.0 + scale) + shift


def _mixer_kernel(x_ref, mod_ref, g_ref, w_in_ref, cw_ref, rg_ref, w_out_ref,
                  cos_ref, sin_ref, dec_ref, xi_ref, zeta_ref, cd_ref,
                  w_up_f32_ref, w_dn_f32_ref,
                  o_ref, w_up_bf16_ref, w_dn_bf16_ref, ubuf, ybuf, state):
    ts = x_ref.shape[0]

    w_up_bf16_ref[...] = w_up_f32_ref[...].astype(BF16)
    w_dn_bf16_ref[...] = w_dn_f32_ref[...].astype(BF16)

    @pl.when(pl.program_id(1) == 0)
    def _start_of_sequence():
        ubuf[0:HALO, :] = jnp.zeros((HALO, CONV_WIDTH), F32)
        state[...] = jnp.zeros(state.shape, F32)

    x = x_ref[...]
    mod = mod_ref[...]
    shift = mod[:, 0:D_MODEL]
    scale = mod[:, D_MODEL:2 * D_MODEL]
    gate = mod[:, 2 * D_MODEL:3 * D_MODEL]
    hb = _modulated_rms(x, g_ref[...], scale, shift).astype(BF16)

    def proj(i):
        return jnp.dot(hb, w_in_ref[:, i * SEG:(i + 1) * SEG],
                       preferred_element_type=F32)

    q = proj(3)
    k = proj(4)
    v = proj(5)

    u = proj(1) * proj(2)
    ubuf[HALO:HALO + ts, :] = u
    u1 = ubuf[HALO - 1:HALO - 1 + ts, :]
    u2 = ubuf[HALO - 2:HALO - 2 + ts, :]
    cw = cw_ref[...]
    conv = u2 * cw[0:1, :] + u1 * cw[1:2, :] + u * cw[2:3, :]
    ubuf[0:HALO, :] = ubuf[ts:ts + HALO, :]

    cos2 = cos_ref[...]
    sin2 = sin_ref[...]
    rc = dec_ref.shape[1]
    n_chunks = ts // rc
    pairs = [(hh, ci) for hh in range(RET_HEADS) for ci in range(n_chunks)]
    head_cols = [slice(hh * HEAD_DIM, (hh + 1) * HEAD_DIM) for hh in range(RET_HEADS)]
    chunk_rows = [slice(ci * rc, (ci + 1) * rc) for ci in range(n_chunks)]
    qh, kh = [], []
    for cols in head_cols:
        qc, kc = q[:, cols], k[:, cols]
        qh.append(qc * cos2 + pltpu.roll(qc, HEAD_DIM // 2, 1) * sin2)
        kh.append(kc * cos2 + pltpu.roll(kc, HEAD_DIM // 2, 1) * sin2)

    gr = proj(6)

    scores_t, kv_t, v_t = {}, {}, {}
    s_now = [state[hh] for hh in range(RET_HEADS)]
    for step in range(len(pairs) + LOOKAHEAD):
        if step < len(pairs):
            hh, ci = pairs[step]
            rows = chunk_rows[ci]
            kc = kh[hh][rows].astype(BF16)
            scores_t[hh, ci] = lax.dot_general(
                kc, qh[hh][rows].astype(BF16),
                (((1,), (1,)), ((), ())), preferred_element_type=F32)
            vt = v[rows, head_cols[hh]].T
            v_t[hh, ci] = vt.astype(BF16)
            kv_t[hh, ci] = jnp.dot((vt * zeta_ref[hh][0:1, :]).astype(BF16), kc,
                                   preferred_element_type=F32)
        if step >= LOOKAHEAD:
            hh, ci = pairs[step - LOOKAHEAD]
            rows, cols = chunk_rows[ci], head_cols[hh]
            p_t = (scores_t.pop((hh, ci)) * dec_ref[hh]).astype(BF16)
            inner_t = jnp.dot(v_t.pop((hh, ci)), p_t, preferred_element_type=F32)
            cross_t = lax.dot_general(
                s_now[hh].astype(BF16), (qh[hh][rows] * xi_ref[hh]).astype(BF16),
                (((1,), (1,)), ((), ())), preferred_element_type=F32)
            s_now[hh] = s_now[hh] * cd_ref[hh][0:1, :] + kv_t.pop((hh, ci))
            o = (inner_t + cross_t).T
            mu = jnp.mean(o, axis=-1, keepdims=True)
            oc = o - mu
            var = jnp.mean(oc * oc, axis=-1, keepdims=True)
            on = (oc * lax.rsqrt(var + EPS)) * rg_ref[:, cols]
            g = gr[rows, cols]
            y = (g * jax.nn.sigmoid(g)) * on
            ybuf[rows, CONV_WIDTH + hh * HEAD_DIM:
                 CONV_WIDTH + (hh + 1) * HEAD_DIM] = y.astype(BF16)
    for hh in range(RET_HEADS):
        state[hh] = s_now[hh]

    cb = proj(0)
    ybuf[:, 0:CONV_WIDTH] = (cb * conv).astype(BF16)

    mix = (jnp.dot(ybuf[:, CONV_WIDTH:], w_out_ref[CONV_WIDTH:, :],
                   preferred_element_type=F32)
           + jnp.dot(ybuf[:, 0:CONV_WIDTH], w_out_ref[0:CONV_WIDTH, :],
                     preferred_element_type=F32))
    o_ref[...] = x + gate * mix


def _cast_rows(n_rows, n_steps):
    rows = n_rows // n_steps
    assert rows * n_steps == n_rows and rows % BF16_SUBLANES == 0
    return rows


def _mixer(layer, x, mod, norm_g, w_in, conv_w, ret_g, w_out, w_up, w_down, tables):
    nb, t, d = x.shape
    cos2, sin2, dec, xi, zeta, cd = tables
    rc = dec.shape[1]
    ts = MIX_TILE
    ns = t // ts
    up_rows, dn_rows = _cast_rows(D_MODEL, nb * ns), _cast_rows(D_FF, nb * ns)
    const2 = lambda b, s: (0, 0)
    const3 = lambda b, s: (0, 0, 0)
    lay2 = lambda b, s: (layer, 0, 0)
    return pl.pallas_call(
        _mixer_kernel,
        grid=(nb, ns),
        in_specs=[
            pl.BlockSpec((None, ts, d), lambda b, s: (b, s, 0)),
            pl.BlockSpec((None, None, 1, N_MOD * d), lambda b, s: (layer, b, 0, 0)),
            _resident((None, 1, d), lay2),
            _resident((d, IN_COLS), const2),
            _resident((None, CONV_K, CONV_WIDTH), lay2),
            _resident((None, 1, RET_WIDTH), lay2),
            _resident((d, d), const2),
            pl.BlockSpec((ts, HEAD_DIM), lambda b, s: (s, 0)),
            pl.BlockSpec((ts, HEAD_DIM), lambda b, s: (s, 0)),
            _resident((RET_HEADS, rc, rc), const3),
            _resident((RET_HEADS, rc, HEAD_DIM), const3),
            _resident((RET_HEADS, 8, rc), const3),
            _resident((RET_HEADS, 8, HEAD_DIM), const3),
            pl.BlockSpec((None, up_rows, D_FF), lambda b, s: (layer, b * ns + s, 0)),
            pl.BlockSpec((None, dn_rows, d), lambda b, s: (layer, b * ns + s, 0)),
        ],
        out_specs=[
            pl.BlockSpec((None, ts, d), lambda b, s: (b, s, 0)),
            pl.BlockSpec((up_rows, D_FF), lambda b, s: (b * ns + s, 0)),
            pl.BlockSpec((dn_rows, d), lambda b, s: (b * ns + s, 0)),
        ],
        out_shape=[
            jax.ShapeDtypeStruct(x.shape, F32),
            jax.ShapeDtypeStruct((D_MODEL, D_FF), BF16),
            jax.ShapeDtypeStruct((D_FF, D_MODEL), BF16),
        ],
        scratch_shapes=[
            pltpu.VMEM((ts + HALO, CONV_WIDTH), F32),
            pltpu.VMEM((ts, d), BF16),
            pltpu.VMEM((RET_HEADS, HEAD_DIM, HEAD_DIM), F32),
        ],
        compiler_params=pltpu.CompilerParams(
            dimension_semantics=("arbitrary", "arbitrary"),
            vmem_limit_bytes=VMEM_LIMIT_BYTES),
        name=f"mixer_{layer}",
    )(x, mod, norm_g, w_in, conv_w, ret_g, w_out, cos2, sin2, dec, xi, zeta, cd,
      w_up, w_down)


def _mlp_kernel(x_ref, mod_ref, g_ref, w_up_ref, w_dn_ref, fg_ref, *rest,
                final, cast_next):
    if cast_next:
        w_in_f32_ref, w_out_f32_ref, o_ref, w_in_bf16_ref, w_out_bf16_ref = rest
        w_in_bf16_ref[...] = w_in_f32_ref[...].astype(BF16)
        w_out_bf16_ref[...] = w_out_f32_ref[...].astype(BF16)
    else:
        o_ref, = rest

    x = x_ref[...]
    mod = mod_ref[...]
    shift = mod[:, 3 * D_MODEL:4 * D_MODEL]
    scale = mod[:, 4 * D_MODEL:5 * D_MODEL]
    gate = mod[:, 5 * D_MODEL:6 * D_MODEL]
    hb = _modulated_rms(x, g_ref[...], scale, shift).astype(BF16)
    acc = None
    for ci in range(D_FF // FF_CHUNK):
        cols = slice(ci * FF_CHUNK, (ci + 1) * FF_CHUNK)
        up = jnp.dot(hb, w_up_ref[:, cols], preferred_element_type=F32)
        act = jnp.square(jnp.maximum(up, 0.0)).astype(BF16)
        down = jnp.dot(act, w_dn_ref[cols, :], preferred_element_type=F32)
        acc = down if acc is None else acc + down
    y = x + gate * acc
    if final:
        r = lax.rsqrt(jnp.mean(y * y, axis=-1, keepdims=True) + EPS)
        y = (y * r) * fg_ref[...]
    o_ref[...] = y


def _mlp(layer, x, mod, norm_g, w_up, w_down, final_g, w_in, w_out):
    nb, t, d = x.shape
    ts = MLP_TILE
    ns = t // ts
    final = layer == DEPTH - 1
    const2 = lambda b, s: (0, 0)
    in_specs = [
        pl.BlockSpec((None, ts, d), lambda b, s: (b, s, 0)),
        pl.BlockSpec((None, None, 1, N_MOD * d), lambda b, s: (layer, b, 0, 0)),
        _resident((None, 1, d), lambda b, s: (layer, 0, 0)),
        _resident((d, D_FF), const2),
        _resident((D_FF, d), const2),
        _resident((1, d), const2),
    ]
    operands = [x, mod, norm_g, w_up, w_down, final_g]
    out_specs = [pl.BlockSpec((None, ts, d), lambda b, s: (b, s, 0))]
    out_shape = [jax.ShapeDtypeStruct(x.shape, F32)]
    if not final:
        rows = _cast_rows(D_MODEL, nb * ns)
        step_rows = lambda b, s: (b * ns + s, 0)
        next_rows = lambda b, s: (layer + 1, b * ns + s, 0)
        in_specs += [pl.BlockSpec((None, rows, IN_COLS), next_rows),
                     pl.BlockSpec((None, rows, d), next_rows)]
        operands += [w_in, w_out]
        out_specs += [pl.BlockSpec((rows, IN_COLS), step_rows),
                      pl.BlockSpec((rows, d), step_rows)]
        out_shape += [jax.ShapeDtypeStruct((D_MODEL, IN_COLS), BF16),
                      jax.ShapeDtypeStruct((D_MODEL, D_MODEL), BF16)]
    return pl.pallas_call(
        functools.partial(_mlp_kernel, final=final, cast_next=not final),
        grid=(nb, ns),
        in_specs=in_specs,
        out_specs=out_specs,
        out_shape=out_shape,
        compiler_params=pltpu.CompilerParams(
            dimension_semantics=("arbitrary", "arbitrary"),
            vmem_limit_bytes=VMEM_LIMIT_BYTES),
        name=f"mlp_{layer}",
    )(*operands)


def _tables(t):
    half = HEAD_DIM // 2
    inv_freq = ROPE_BASE ** (-np.arange(0, HEAD_DIM, 2, dtype=np.float64) / HEAD_DIM)
    ang = np.arange(t, dtype=np.float64)[:, None] * inv_freq[None, :]
    cos, sin = np.cos(ang), np.sin(ang)
    assert cos.shape == (t, half)
    cos2 = np.concatenate([cos, cos], axis=-1)
    sin2 = np.concatenate([-sin, sin], axis=-1)

    rc = RET_CHUNK
    k_scale = HEAD_DIM ** -0.5
    log_g = np.log1p(-np.exp2(-5.0 - np.arange(RET_HEADS, dtype=np.float64)))
    idx = np.arange(rc, dtype=np.float64)
    diff = idx[:, None] - idx[None, :]
    causal = diff >= 0
    dec = np.where(causal[None],
                   np.exp(log_g[:, None, None] * np.where(causal, diff, 0.0)[None]),
                   0.0) * k_scale
    xi = np.exp(log_g[:, None] * (idx + 1.0)[None])
    zeta = np.exp(log_g[:, None] * (rc - 1.0 - idx)[None]) * k_scale
    cd = np.exp(log_g * rc)
    lanes = (RET_HEADS, rc, HEAD_DIM)
    tables = (cos2, sin2, np.swapaxes(dec, 1, 2),
              np.broadcast_to(xi[:, :, None], lanes),
              np.broadcast_to(zeta[:, None, :], (RET_HEADS, 8, rc)),
              np.broadcast_to(cd[:, None, None], (RET_HEADS, 8, HEAD_DIM)))
    return tuple(jnp.asarray(a, dtype=F32) for a in tables)


def kernel(x, c, norm1_g, w_in, conv_w, ret_norm_g, w_out, norm2_g, w_up, w_down,
           w_ada, b_ada, final_g):
    nb, t, d = x.shape
    assert d == D_MODEL and t % MIX_TILE == 0 and t % MLP_TILE == 0
    assert MIX_TILE % RET_CHUNK == 0
    assert w_in.shape == (DEPTH, D_MODEL, IN_COLS)
    mod, w_in_b, w_out_b = _adaln(c, w_ada, b_ada, w_in, w_out)
    mod = mod.reshape(DEPTH, nb, 1, N_MOD * D_MODEL)
    tables = _tables(t)
    g1 = norm1_g.reshape(DEPTH, 1, D_MODEL)
    g2 = norm2_g.reshape(DEPTH, 1, D_MODEL)
    rg = ret_norm_g.reshape(DEPTH, 1, RET_WIDTH)
    fg = final_g.reshape(1, D_MODEL)
    for layer in range(DEPTH):
        x, w_up_b, w_down_b = _mixer(layer, x, mod, g1, w_in_b, conv_w, rg, w_out_b,
                                     w_up, w_down, tables)
        outs = _mlp(layer, x, mod, g2, w_up_b, w_down_b, fg, w_in, w_out)
        if layer < DEPTH - 1:
            x, w_in_b, w_out_b = outs
        else:
            x, = outs
    return x
```

```python
import functools

import jax
import jax.numpy as jnp
import numpy as np
from jax import lax
from jax.experimental import pallas as pl
from jax.experimental.pallas import tpu as pltpu

D_MODEL = 1024
DEPTH = 4
CONV_WIDTH = 512
CONV_K = 3
RET_HEADS = 4
RET_WIDTH = 512
HEAD_DIM = 128
D_FF = 4 * D_MODEL
ROPE_BASE = 10000.0
EPS = 1e-6
N_MOD = 6
SEG = 512
IN_COLS = 7 * SEG

MIX_TILE = 1024
MLP_TILE = 1024
RET_CHUNK = 256
FF_CHUNK = 1024
ADA_COLS = 1536
HALO = 8
LOOKAHEAD = 2
VMEM_LIMIT_BYTES = 56 * 1024 * 1024

F32 = jnp.float32
BF16 = jnp.bfloat16
BF16_SUBLANES = 16


def _resident(block_shape, index_map):
    return pl.BlockSpec(block_shape, index_map, pipeline_mode=pl.Buffered(1))


def _adaln_kernel(c_ref, w_ref, b_ref, w_in_f32_ref, w_out_f32_ref,
                  o_ref, w_in_bf16_ref, w_out_bf16_ref):
    w_in_bf16_ref[...] = w_in_f32_ref[...].astype(BF16)
    w_out_bf16_ref[...] = w_out_f32_ref[...].astype(BF16)

    c = c_ref[...]
    ca = c * jax.nn.sigmoid(c)
    ca_hi = ca.astype(BF16).astype(F32)
    lhs = jnp.concatenate([ca_hi, ca - ca_hi], axis=0).astype(BF16)
    w = w_ref[...]
    w_hi = w.astype(BF16)
    w_lo = (w - w_hi.astype(F32)).astype(BF16)
    acc = (jnp.dot(lhs, w_hi, preferred_element_type=F32)
           + jnp.dot(lhs, w_lo, preferred_element_type=F32))
    nb = c.shape[0]
    bias = b_ref[pl.ds(pl.program_id(0), 1), :]
    o_ref[...] = acc[0:nb] + acc[nb:2 * nb] + bias


def _adaln(c, w_ada, b_ada, w_in, w_out):
    nb = c.shape[0]
    n_col = N_MOD * D_MODEL // ADA_COLS
    rows = _cast_rows(D_MODEL, DEPTH * n_col)
    step_rows = lambda l, j: (l * n_col + j, 0)
    first_rows = lambda l, j: (0, l * n_col + j, 0)
    return pl.pallas_call(
        _adaln_kernel,
        grid=(DEPTH, n_col),
        in_specs=[
            pl.BlockSpec((nb, D_MODEL), lambda l, j: (0, 0)),
            pl.BlockSpec((None, D_MODEL, ADA_COLS), lambda l, j: (l, 0, j)),
            pl.BlockSpec((DEPTH, ADA_COLS), lambda l, j: (0, j)),
            pl.BlockSpec((None, rows, IN_COLS), first_rows),
            pl.BlockSpec((None, rows, D_MODEL), first_rows),
        ],
        out_specs=[
            pl.BlockSpec((None, nb, ADA_COLS), lambda l, j: (l, 0, j)),
            pl.BlockSpec((rows, IN_COLS), step_rows),
            pl.BlockSpec((rows, D_MODEL), step_rows),
        ],
        out_shape=[
            jax.ShapeDtypeStruct((DEPTH, nb, N_MOD * D_MODEL), F32),
            jax.ShapeDtypeStruct((D_MODEL, IN_COLS), BF16),
            jax.ShapeDtypeStruct((D_MODEL, D_MODEL), BF16),
        ],
        compiler_params=pltpu.CompilerParams(
            dimension_semantics=("arbitrary", "arbitrary"),
            vmem_limit_bytes=VMEM_LIMIT_BYTES),
        name="adaln_mod",
    )(c, w_ada, b_ada, w_in, w_out)


def _modulated_rms(x, gain, scale, shift):
    r = lax.rsqrt(jnp.mean(x * x, axis=-1, keepdims=True) + EPS)
    return ((x * r) * gain) * (1.0 + scale) + shift


def _mixer_kernel(x_ref, mod_ref, g_ref, w_in_ref, cw_ref, rg_ref, w_out_ref,
                  cos_ref, sin_ref, dec_ref, xi_ref, zeta_ref, cd_ref,
                  w_up_f32_ref, w_dn_f32_ref,
                  o_ref, w_up_bf16_ref, w_dn_bf16_ref, ubuf, ybuf, state, *, layer):
    ts = x_ref.shape[0]

    @pl.when(pl.program_id(1) == 0)
    def _start_of_sequence():
        ubuf[0:HALO, :] = jnp.zeros((HALO, CONV_WIDTH), F32)
        state[...] = jnp.zeros(state.shape, F32)

    x = x_ref[...]
    mod = mod_ref[pl.ds(pl.program_id(0), 1), :]
    shift = mod[:, 0:D_MODEL]
    scale = mod[:, D_MODEL:2 * D_MODEL]
    gate = mod[:, 2 * D_MODEL:3 * D_MODEL]
    hb = _modulated_rms(x, g_ref[pl.ds(layer, 1), :], scale, shift).astype(BF16)

    def proj(i):
        return jnp.dot(hb, w_in_ref[:, i * SEG:(i + 1) * SEG],
                       preferred_element_type=F32)

    q = proj(3)
    k = proj(4)
    v = proj(5)

    u = proj(1) * proj(2)
    ubuf[HALO:HALO + ts, :] = u
    u1 = ubuf[HALO - 1:HALO - 1 + ts, :]
    u2 = ubuf[HALO - 2:HALO - 2 + ts, :]
    cw = cw_ref[...]
    conv = u2 * cw[0:1, :] + u1 * cw[1:2, :] + u * cw[2:3, :]
    ubuf[0:HALO, :] = ubuf[ts:ts + HALO, :]

    cos2 = cos_ref[...]
    sin2 = sin_ref[...]
    rc = dec_ref.shape[1]
    n_chunks = ts // rc
    pairs = [(hh, ci) for hh in range(RET_HEADS) for ci in range(n_chunks)]
    head_cols = [slice(hh * HEAD_DIM, (hh + 1) * HEAD_DIM) for hh in range(RET_HEADS)]
    chunk_rows = [slice(ci * rc, (ci + 1) * rc) for ci in range(n_chunks)]
    qh, kh = [], []
    for cols in head_cols:
        qc, kc = q[:, cols], k[:, cols]
        qh.append(qc * cos2 + pltpu.roll(qc, HEAD_DIM // 2, 1) * sin2)
        kh.append(kc * cos2 + pltpu.roll(kc, HEAD_DIM // 2, 1) * sin2)

    gr = proj(6)

    scores_t, kv_t, v_t = {}, {}, {}
    s_now = [state[hh] for hh in range(RET_HEADS)]
    for step in range(len(pairs) + LOOKAHEAD):
        if step < len(pairs):
            hh, ci = pairs[step]
            rows = chunk_rows[ci]
            kc = kh[hh][rows].astype(BF16)
            scores_t[hh, ci] = lax.dot_general(
                kc, qh[hh][rows].astype(BF16),
                (((1,), (1,)), ((), ())), preferred_element_type=F32)
            vt = v[rows, head_cols[hh]].T
            v_t[hh, ci] = vt.astype(BF16)
            kv_t[hh, ci] = jnp.dot((vt * zeta_ref[hh][0:1, :]).astype(BF16), kc,
                                   preferred_element_type=F32)
        if step >= LOOKAHEAD:
            hh, ci = pairs[step - LOOKAHEAD]
            rows, cols = chunk_rows[ci], head_cols[hh]
            p_t = (scores_t.pop((hh, ci)) * dec_ref[hh]).astype(BF16)
            inner_t = jnp.dot(v_t.pop((hh, ci)), p_t, preferred_element_type=F32)
            cross_t = lax.dot_general(
                s_now[hh].astype(BF16), (qh[hh][rows] * xi_ref[hh]).astype(BF16),
                (((1,), (1,)), ((), ())), preferred_element_type=F32)
            s_now[hh] = s_now[hh] * cd_ref[hh][0:1, :] + kv_t.pop((hh, ci))
            o = (inner_t + cross_t).T
            mu = jnp.mean(o, axis=-1, keepdims=True)
            oc = o - mu
            var = jnp.mean(oc * oc, axis=-1, keepdims=True)
            on = (oc * lax.rsqrt(var + EPS)) * rg_ref[pl.ds(layer, 1), cols]
            g = gr[rows, cols]
            y = (g * jax.nn.sigmoid(g)) * on
            ybuf[rows, CONV_WIDTH + hh * HEAD_DIM:
                 CONV_WIDTH + (hh + 1) * HEAD_DIM] = y.astype(BF16)
    for hh in range(RET_HEADS):
        state[hh] = s_now[hh]

    cb = proj(0)
    ybuf[:, 0:CONV_WIDTH] = (cb * conv).astype(BF16)

    mix = (jnp.dot(ybuf[:, CONV_WIDTH:], w_out_ref[CONV_WIDTH:, :],
                   preferred_element_type=F32)
           + jnp.dot(ybuf[:, 0:CONV_WIDTH], w_out_ref[0:CONV_WIDTH, :],
                     preferred_element_type=F32))
    o_ref[...] = x + gate * mix

    w_up_bf16_ref[...] = w_up_f32_ref[...].astype(BF16)
    w_dn_bf16_ref[...] = w_dn_f32_ref[...].astype(BF16)


def _cast_rows(n_rows, n_steps):
    rows = n_rows // n_steps
    assert rows * n_steps == n_rows and rows % BF16_SUBLANES == 0
    return rows


def _mixer(layer, x, mod, norm_g, w_in, conv_w, ret_g, w_out, w_up, w_down, tables):
    nb, t, d = x.shape
    cos2, sin2, dec, xi, zeta, cd = tables
    rc = dec.shape[1]
    ts = MIX_TILE
    ns = t // ts
    up_rows, dn_rows = _cast_rows(D_MODEL, nb * ns), _cast_rows(D_FF, nb * ns)
    const2 = lambda b, s: (0, 0)
    const3 = lambda b, s: (0, 0, 0)
    lay2 = lambda b, s: (layer, 0, 0)
    return pl.pallas_call(
        functools.partial(_mixer_kernel, layer=layer),
        grid=(nb, ns),
        in_specs=[
            pl.BlockSpec((None, ts, d), lambda b, s: (b, s, 0)),
            _resident((None, nb, N_MOD * d), lay2),
            _resident((DEPTH, d), const2),
            _resident((d, IN_COLS), const2),
            _resident((None, CONV_K, CONV_WIDTH), lay2),
            _resident((DEPTH, RET_WIDTH), const2),
            _resident((d, d), const2),
            pl.BlockSpec((ts, HEAD_DIM), lambda b, s: (s, 0)),
            pl.BlockSpec((ts, HEAD_DIM), lambda b, s: (s, 0)),
            _resident((RET_HEADS, rc, rc), const3),
            _resident((RET_HEADS, rc, HEAD_DIM), const3),
            _resident((RET_HEADS, 8, rc), const3),
            _resident((RET_HEADS, 8, HEAD_DIM), const3),
            pl.BlockSpec((None, up_rows, D_FF), lambda b, s: (layer, b * ns + s, 0)),
            pl.BlockSpec((None, dn_rows, d), lambda b, s: (layer, b * ns + s, 0)),
        ],
        out_specs=[
            pl.BlockSpec((None, ts, d), lambda b, s: (b, s, 0)),
            pl.BlockSpec((up_rows, D_FF), lambda b, s: (b * ns + s, 0)),
            pl.BlockSpec((dn_rows, d), lambda b, s: (b * ns + s, 0)),
        ],
        out_shape=[
            jax.ShapeDtypeStruct(x.shape, F32),
            jax.ShapeDtypeStruct((D_MODEL, D_FF), BF16),
            jax.ShapeDtypeStruct((D_FF, D_MODEL), BF16),
        ],
        scratch_shapes=[
            pltpu.VMEM((ts + HALO, CONV_WIDTH), F32),
            pltpu.VMEM((ts, d), BF16),
            pltpu.VMEM((RET_HEADS, HEAD_DIM, HEAD_DIM), F32),
        ],
        compiler_params=pltpu.CompilerParams(
            dimension_semantics=("arbitrary", "arbitrary"),
            vmem_limit_bytes=VMEM_LIMIT_BYTES),
        name=f"mixer_{layer}",
    )(x, mod, norm_g, w_in, conv_w, ret_g, w_out, cos2, sin2, dec, xi, zeta, cd,
      w_up, w_down)


def _mlp_kernel(x_ref, mod_ref, g_ref, w_up_ref, w_dn_ref, fg_ref, *rest,
                layer, final, cast_next):
    o_ref = rest[2] if cast_next else rest[0]
    x = x_ref[...]
    mod = mod_ref[pl.ds(pl.program_id(0), 1), :]
    shift = mod[:, 3 * D_MODEL:4 * D_MODEL]
    scale = mod[:, 4 * D_MODEL:5 * D_MODEL]
    gate = mod[:, 5 * D_MODEL:6 * D_MODEL]
    hb = _modulated_rms(x, g_ref[pl.ds(layer, 1), :], scale, shift).astype(BF16)
    acc = None
    for ci in range(D_FF // FF_CHUNK):
        cols = slice(ci * FF_CHUNK, (ci + 1) * FF_CHUNK)
        up = jnp.dot(hb, w_up_ref[:, cols], preferred_element_type=F32)
        act = jnp.square(jnp.maximum(up, 0.0)).astype(BF16)
        down = jnp.dot(act, w_dn_ref[cols, :], preferred_element_type=F32)
        acc = down if acc is None else acc + down
    y = x + gate * acc
    if final:
        r = lax.rsqrt(jnp.mean(y * y, axis=-1, keepdims=True) + EPS)
        y = (y * r) * fg_ref[...]
    o_ref[...] = y

    if cast_next:
        w_in_f32_ref, w_out_f32_ref, _, w_in_bf16_ref, w_out_bf16_ref = rest
        w_in_bf16_ref[...] = w_in_f32_ref[...].astype(BF16)
        w_out_bf16_ref[...] = w_out_f32_ref[...].astype(BF16)


def _mlp(layer, x, mod, norm_g, w_up, w_down, final_g, w_in, w_out):
    nb, t, d = x.shape
    ts = MLP_TILE
    ns = t // ts
    final = layer == DEPTH - 1
    const2 = lambda b, s: (0, 0)
    in_specs = [
        pl.BlockSpec((None, ts, d), lambda b, s: (b, s, 0)),
        _resident((None, nb, N_MOD * d), lambda b, s: (layer, 0, 0)),
        _resident((DEPTH, d), const2),
        _resident((d, D_FF), const2),
        _resident((D_FF, d), const2),
        _resident((1, d), const2),
    ]
    operands = [x, mod, norm_g, w_up, w_down, final_g]
    out_specs = [pl.BlockSpec((None, ts, d), lambda b, s: (b, s, 0))]
    out_shape = [jax.ShapeDtypeStruct(x.shape, F32)]
    if not final:
        rows = _cast_rows(D_MODEL, nb * ns)
        step_rows = lambda b, s: (b * ns + s, 0)
        next_rows = lambda b, s: (layer + 1, b * ns + s, 0)
        in_specs += [pl.BlockSpec((None, rows, IN_COLS), next_rows),
                     pl.BlockSpec((None, rows, d), next_rows)]
        operands += [w_in, w_out]
        out_specs += [pl.BlockSpec((rows, IN_COLS), step_rows),
                      pl.BlockSpec((rows, d), step_rows)]
        out_shape += [jax.ShapeDtypeStruct((D_MODEL, IN_COLS), BF16),
                      jax.ShapeDtypeStruct((D_MODEL, D_MODEL), BF16)]
    return pl.pallas_call(
        functools.partial(_mlp_kernel, layer=layer, final=final, cast_next=not final),
        grid=(nb, ns),
        in_specs=in_specs,
        out_specs=out_specs,
        out_shape=out_shape,
        compiler_params=pltpu.CompilerParams(
            dimension_semantics=("arbitrary", "arbitrary"),
            vmem_limit_bytes=VMEM_LIMIT_BYTES),
        name=f"mlp_{layer}",
    )(*operands)


def _tables(t):
    half = HEAD_DIM // 2
    inv_freq = ROPE_BASE ** (-np.arange(0, HEAD_DIM, 2, dtype=np.float64) / HEAD_DIM)
    ang = np.arange(t, dtype=np.float64)[:, None] * inv_freq[None, :]
    cos, sin = np.cos(ang), np.sin(ang)
    assert cos.shape == (t, half)
    cos2 = np.concatenate([cos, cos], axis=-1)
    sin2 = np.concatenate([-sin, sin], axis=-1)

    rc = RET_CHUNK
    k_scale = HEAD_DIM ** -0.5
    log_g = np.log1p(-np.exp2(-5.0 - np.arange(RET_HEADS, dtype=np.float64)))
    idx = np.arange(rc, dtype=np.float64)
    diff = idx[:, None] - idx[None, :]
    causal = diff >= 0
    dec = np.where(causal[None],
                   np.exp(log_g[:, None, None] * np.where(causal, diff, 0.0)[None]),
                   0.0) * k_scale
    xi = np.exp(log_g[:, None] * (idx + 1.0)[None])
    zeta = np.exp(log_g[:, None] * (rc - 1.0 - idx)[None]) * k_scale
    cd = np.exp(log_g * rc)
    lanes = (RET_HEADS, rc, HEAD_DIM)
    tables = (cos2, sin2, np.swapaxes(dec, 1, 2),
              np.broadcast_to(xi[:, :, None], lanes),
              np.broadcast_to(zeta[:, None, :], (RET_HEADS, 8, rc)),
              np.broadcast_to(cd[:, None, None], (RET_HEADS, 8, HEAD_DIM)))
    return tuple(jnp.asarray(a, dtype=F32) for a in tables)


def kernel(x, c, norm1_g, w_in, conv_w, ret_norm_g, w_out, norm2_g, w_up, w_down,
           w_ada, b_ada, final_g):
    nb, t, d = x.shape
    assert d == D_MODEL and t % MIX_TILE == 0 and t % MLP_TILE == 0
    assert MIX_TILE % RET_CHUNK == 0
    assert w_in.shape == (DEPTH, D_MODEL, IN_COLS)
    mod, w_in_b, w_out_b = _adaln(c, w_ada, b_ada, w_in, w_out)
    tables = _tables(t)
    fg = final_g.reshape(1, D_MODEL)
    for layer in range(DEPTH):
        x, w_up_b, w_down_b = _mixer(layer, x, mod, norm1_g, w_in_b, conv_w, ret_norm_g,
                                     w_out_b, w_up, w_down, tables)
        outs = _mlp(layer, x, mod, norm2_g, w_up_b, w_down_b, fg, w_in, w_out)
        if layer < DEPTH - 1:
            x, w_in_b, w_out_b = outs
        else:
            x, = outs
    return x
```

```python
import functools

import jax
import jax.numpy as jnp
import numpy as np
from jax import lax
from jax.experimental import pallas as pl
from jax.experimental.pallas import tpu as pltpu

D_MODEL = 1024
DEPTH = 4
CONV_WIDTH = 512
CONV_K = 3
RET_HEADS = 4
RET_WIDTH = 512
HEAD_DIM = 128
D_FF = 4 * D_MODEL
ROPE_BASE = 10000.0
EPS = 1e-6
N_MOD = 6
SEG = 512
IN_COLS = 7 * SEG

MIX_TILE = 1024
MLP_TILE = 1024
RET_CHUNK = 256
FF_CHUNK = 1024
ADA_COLS = 1536
HALO = 8
LOOKAHEAD = 2
HEAD_ROW_CHUNKS = 4
VMEM_LIMIT_BYTES = 56 * 1024 * 1024

F32 = jnp.float32
BF16 = jnp.bfloat16
BF16_SUBLANES = 16


def _resident(block_shape, index_map):
    return pl.BlockSpec(block_shape, index_map, pipeline_mode=pl.Buffered(1))


def _adaln_kernel(c_ref, w_ref, b_ref, w_in_f32_ref, w_out_f32_ref,
                  o_ref, w_in_bf16_ref, w_out_bf16_ref):
    w_in_bf16_ref[...] = w_in_f32_ref[...].astype(BF16)
    w_out_bf16_ref[...] = w_out_f32_ref[...].astype(BF16)

    c = c_ref[...]
    ca = c * jax.nn.sigmoid(c)
    ca_hi = ca.astype(BF16).astype(F32)
    lhs = jnp.concatenate([ca_hi, ca - ca_hi], axis=0).astype(BF16)
    w = w_ref[...]
    w_hi = w.astype(BF16)
    w_lo = (w - w_hi.astype(F32)).astype(BF16)
    acc = (jnp.dot(lhs, w_hi, preferred_element_type=F32)
           + jnp.dot(lhs, w_lo, preferred_element_type=F32))
    nb = c.shape[0]
    bias = b_ref[pl.ds(pl.program_id(0), 1), :]
    o_ref[...] = acc[0:nb] + acc[nb:2 * nb] + bias


def _adaln(c, w_ada, b_ada, w_in, w_out):
    nb = c.shape[0]
    n_col = N_MOD * D_MODEL // ADA_COLS
    rows = _cast_rows(D_MODEL, DEPTH * n_col)
    step_rows = lambda l, j: (l * n_col + j, 0)
    first_rows = lambda l, j: (0, l * n_col + j, 0)
    return pl.pallas_call(
        _adaln_kernel,
        grid=(DEPTH, n_col),
        in_specs=[
            pl.BlockSpec((nb, D_MODEL), lambda l, j: (0, 0)),
            pl.BlockSpec((None, D_MODEL, ADA_COLS), lambda l, j: (l, 0, j)),
            pl.BlockSpec((DEPTH, ADA_COLS), lambda l, j: (0, j)),
            pl.BlockSpec((None, rows, IN_COLS), first_rows),
            pl.BlockSpec((None, rows, D_MODEL), first_rows),
        ],
        out_specs=[
            pl.BlockSpec((None, nb, ADA_COLS), lambda l, j: (l, 0, j)),
            pl.BlockSpec((rows, IN_COLS), step_rows),
            pl.BlockSpec((rows, D_MODEL), step_rows),
        ],
        out_shape=[
            jax.ShapeDtypeStruct((DEPTH, nb, N_MOD * D_MODEL), F32),
            jax.ShapeDtypeStruct((D_MODEL, IN_COLS), BF16),
            jax.ShapeDtypeStruct((D_MODEL, D_MODEL), BF16),
        ],
        compiler_params=pltpu.CompilerParams(
            dimension_semantics=("arbitrary", "arbitrary"),
            vmem_limit_bytes=VMEM_LIMIT_BYTES),
        name="adaln_mod",
    )(c, w_ada, b_ada, w_in, w_out)


def _modulated_rms(x, gain, scale, shift):
    r = lax.rsqrt(jnp.mean(x * x, axis=-1, keepdims=True) + EPS)
    return ((x * r) * gain) * (1.0 + scale) + shift


def _mixer_kernel(x_ref, mod_ref, g_ref, w_in_ref, cw_ref, rg_ref, w_out_ref,
                  cos_ref, sin_ref, dec_ref, xi_ref, zeta_ref, cd_ref,
                  w_up_f32_ref, w_dn_f32_ref,
                  o_ref, w_up_bf16_ref, w_dn_bf16_ref, ubuf, ybuf, state, *, layer):
    ts = x_ref.shape[0]

    @pl.when(pl.program_id(1) == 0)
    def _start_of_sequence():
        ubuf[0:HALO, :] = jnp.zeros((HALO, CONV_WIDTH), F32)
        state[...] = jnp.zeros(state.shape, F32)

    x = x_ref[...]
    mod = mod_ref[pl.ds(pl.program_id(0), 1), :]
    shift = mod[:, 0:D_MODEL]
    scale = mod[:, D_MODEL:2 * D_MODEL]
    gate = mod[:, 2 * D_MODEL:3 * D_MODEL]
    hb = _modulated_rms(x, g_ref[pl.ds(layer, 1), :], scale, shift).astype(BF16)

    def proj(i):
        return jnp.dot(hb, w_in_ref[:, i * SEG:(i + 1) * SEG],
                       preferred_element_type=F32)

    q = proj(3)
    k = proj(4)
    v = proj(5)

    u = proj(1) * proj(2)
    ubuf[HALO:HALO + ts, :] = u
    u1 = ubuf[HALO - 1:HALO - 1 + ts, :]
    u2 = ubuf[HALO - 2:HALO - 2 + ts, :]
    cw = cw_ref[...]
    conv = u2 * cw[0:1, :] + u1 * cw[1:2, :] + u * cw[2:3, :]
    ubuf[0:HALO, :] = ubuf[ts:ts + HALO, :]

    cos2 = cos_ref[...]
    sin2 = sin_ref[...]
    rc = dec_ref.shape[1]
    n_chunks = ts // rc
    pairs = [(hh, ci) for hh in range(RET_HEADS) for ci in range(n_chunks)]
    head_cols = [slice(hh * HEAD_DIM, (hh + 1) * HEAD_DIM) for hh in range(RET_HEADS)]
    chunk_rows = [slice(ci * rc, (ci + 1) * rc) for ci in range(n_chunks)]
    qh, kh = [], []
    for cols in head_cols:
        qc, kc = q[:, cols], k[:, cols]
        qh.append(qc * cos2 + pltpu.roll(qc, HEAD_DIM // 2, 1) * sin2)
        kh.append(kc * cos2 + pltpu.roll(kc, HEAD_DIM // 2, 1) * sin2)

    gr = proj(6)

    scores_t, kv_t, v_t = {}, {}, {}
    s_now = [state[hh] for hh in range(RET_HEADS)]
    for step in range(len(pairs) + LOOKAHEAD):
        if step < len(pairs):
            hh, ci = pairs[step]
            rows = chunk_rows[ci]
            kc = kh[hh][rows].astype(BF16)
            scores_t[hh, ci] = lax.dot_general(
                kc, qh[hh][rows].astype(BF16),
                (((1,), (1,)), ((), ())), preferred_element_type=F32)
            vt = v[rows, head_cols[hh]].T
            v_t[hh, ci] = vt.astype(BF16)
            kv_t[hh, ci] = jnp.dot((vt * zeta_ref[hh][0:1, :]).astype(BF16), kc,
                                   preferred_element_type=F32)
        if step >= LOOKAHEAD:
            hh, ci = pairs[step - LOOKAHEAD]
            rows, cols = chunk_rows[ci], head_cols[hh]
            p_t = (scores_t.pop((hh, ci)) * dec_ref[hh]).astype(BF16)
            inner_t = jnp.dot(v_t.pop((hh, ci)), p_t, preferred_element_type=F32)
            cross_t = lax.dot_general(
                s_now[hh].astype(BF16), (qh[hh][rows] * xi_ref[hh]).astype(BF16),
                (((1,), (1,)), ((), ())), preferred_element_type=F32)
            s_now[hh] = s_now[hh] * cd_ref[hh][0:1, :] + kv_t.pop((hh, ci))
            o = (inner_t + cross_t).T
            mu = jnp.mean(o, axis=-1, keepdims=True)
            oc = o - mu
            var = jnp.mean(oc * oc, axis=-1, keepdims=True)
            on = (oc * lax.rsqrt(var + EPS)) * rg_ref[pl.ds(layer, 1), cols]
            g = gr[rows, cols]
            y = (g * jax.nn.sigmoid(g)) * on
            ybuf[rows, CONV_WIDTH + hh * HEAD_DIM:
                 CONV_WIDTH + (hh + 1) * HEAD_DIM] = y.astype(BF16)
    for hh in range(RET_HEADS):
        state[hh] = s_now[hh]

    cb = proj(0)
    ybuf[:, 0:CONV_WIDTH] = (cb * conv).astype(BF16)

    mix = (jnp.dot(ybuf[:, CONV_WIDTH:], w_out_ref[CONV_WIDTH:, :],
                   preferred_element_type=F32)
           + jnp.dot(ybuf[:, 0:CONV_WIDTH], w_out_ref[0:CONV_WIDTH, :],
                     preferred_element_type=F32))
    o_ref[...] = x + gate * mix

    w_up_bf16_ref[...] = w_up_f32_ref[...].astype(BF16)
    w_dn_bf16_ref[...] = w_dn_f32_ref[...].astype(BF16)


def _cast_rows(n_rows, n_steps):
    rows = n_rows // n_steps
    assert rows * n_steps == n_rows and rows % BF16_SUBLANES == 0
    return rows


def _mixer(layer, x, mod, norm_g, w_in, conv_w, ret_g, w_out, w_up, w_down, tables):
    nb, t, d = x.shape
    cos2, sin2, dec, xi, zeta, cd = tables
    rc = dec.shape[1]
    ts = MIX_TILE
    ns = t // ts
    up_rows, dn_rows = _cast_rows(D_MODEL, nb * ns), _cast_rows(D_FF, nb * ns)
    const2 = lambda b, s: (0, 0)
    const3 = lambda b, s: (0, 0, 0)
    lay2 = lambda b, s: (layer, 0, 0)
    return pl.pallas_call(
        functools.partial(_mixer_kernel, layer=layer),
        grid=(nb, ns),
        in_specs=[
            pl.BlockSpec((None, ts, d), lambda b, s: (b, s, 0)),
            _resident((None, nb, N_MOD * d), lay2),
            _resident((DEPTH, d), const2),
            _resident((d, IN_COLS), const2),
            _resident((None, CONV_K, CONV_WIDTH), lay2),
            _resident((DEPTH, RET_WIDTH), const2),
            _resident((d, d), const2),
            pl.BlockSpec((ts, HEAD_DIM), lambda b, s: (s, 0)),
            pl.BlockSpec((ts, HEAD_DIM), lambda b, s: (s, 0)),
            _resident((RET_HEADS, rc, rc), const3),
            _resident((RET_HEADS, rc, HEAD_DIM), const3),
            _resident((RET_HEADS, 8, rc), const3),
            _resident((RET_HEADS, 8, HEAD_DIM), const3),
            pl.BlockSpec((None, up_rows, D_FF), lambda b, s: (layer, b * ns + s, 0)),
            pl.BlockSpec((None, dn_rows, d), lambda b, s: (layer, b * ns + s, 0)),
        ],
        out_specs=[
            pl.BlockSpec((None, ts, d), lambda b, s: (b, s, 0)),
            pl.BlockSpec((up_rows, D_FF), lambda b, s: (b * ns + s, 0)),
            pl.BlockSpec((dn_rows, d), lambda b, s: (b * ns + s, 0)),
        ],
        out_shape=[
            jax.ShapeDtypeStruct(x.shape, F32),
            jax.ShapeDtypeStruct((D_MODEL, D_FF), BF16),
            jax.ShapeDtypeStruct((D_FF, D_MODEL), BF16),
        ],
        scratch_shapes=[
            pltpu.VMEM((ts + HALO, CONV_WIDTH), F32),
            pltpu.VMEM((ts, d), BF16),
            pltpu.VMEM((RET_HEADS, HEAD_DIM, HEAD_DIM), F32),
        ],
        compiler_params=pltpu.CompilerParams(
            dimension_semantics=("arbitrary", "arbitrary"),
            vmem_limit_bytes=VMEM_LIMIT_BYTES),
        name=f"mixer_{layer}",
    )(x, mod, norm_g, w_in, conv_w, ret_g, w_out, cos2, sin2, dec, xi, zeta, cd,
      w_up, w_down)


def _mlp_kernel(x_ref, mod_ref, g_ref, w_up_ref, w_dn_ref, fg_ref, *rest,
                layer, final, cast_next):
    o_ref = rest[2] if cast_next else rest[0]
    x = x_ref[...]
    mod = mod_ref[pl.ds(pl.program_id(0), 1), :]
    shift = mod[:, 3 * D_MODEL:4 * D_MODEL]
    scale = mod[:, 4 * D_MODEL:5 * D_MODEL]
    gate = mod[:, 5 * D_MODEL:6 * D_MODEL]
    gain = g_ref[pl.ds(layer, 1), :]
    ts = x.shape[0]
    row_chunks = [slice(i * (ts // HEAD_ROW_CHUNKS), (i + 1) * (ts // HEAD_ROW_CHUNKS))
                  for i in range(HEAD_ROW_CHUNKS)]
    hb_rows = [_modulated_rms(x[rows], gain, scale, shift).astype(BF16)
               for rows in row_chunks]
    hb = jnp.concatenate(hb_rows, axis=0)
    acc = None
    for ci in range(D_FF // FF_CHUNK):
        cols = slice(ci * FF_CHUNK, (ci + 1) * FF_CHUNK)
        if ci == 0:
            up = jnp.concatenate(
                [jnp.dot(h, w_up_ref[:, cols], preferred_element_type=F32)
                 for h in hb_rows], axis=0)
        else:
            up = jnp.dot(hb, w_up_ref[:, cols], preferred_element_type=F32)
        act = jnp.square(jnp.maximum(up, 0.0)).astype(BF16)
        if final and ci == D_FF // FF_CHUNK - 1:
            down = jnp.concatenate(
                [jnp.dot(act[rows], w_dn_ref[cols, :], preferred_element_type=F32)
                 for rows in row_chunks], axis=0)
        else:
            down = jnp.dot(act, w_dn_ref[cols, :], preferred_element_type=F32)
        acc = down if acc is None else acc + down
    y = x + gate * acc
    if final:
        r = lax.rsqrt(jnp.mean(y * y, axis=-1, keepdims=True) + EPS)
        y = (y * r) * fg_ref[...]
    o_ref[...] = y

    if cast_next:
        w_in_f32_ref, w_out_f32_ref, _, w_in_bf16_ref, w_out_bf16_ref = rest
        w_in_bf16_ref[...] = w_in_f32_ref[...].astype(BF16)
        w_out_bf16_ref[...] = w_out_f32_ref[...].astype(BF16)


def _mlp(layer, x, mod, norm_g, w_up, w_down, final_g, w_in, w_out):
    nb, t, d = x.shape
    ts = MLP_TILE
    ns = t // ts
    final = layer == DEPTH - 1
    const2 = lambda b, s: (0, 0)
    in_specs = [
        pl.BlockSpec((None, ts, d), lambda b, s: (b, s, 0)),
        _resident((None, nb, N_MOD * d), lambda b, s: (layer, 0, 0)),
        _resident((DEPTH, d), const2),
        _resident((d, D_FF), const2),
        _resident((D_FF, d), const2),
        _resident((1, d), const2),
    ]
    operands = [x, mod, norm_g, w_up, w_down, final_g]
    out_specs = [pl.BlockSpec((None, ts, d), lambda b, s: (b, s, 0))]
    out_shape = [jax.ShapeDtypeStruct(x.shape, F32)]
    if not final:
        rows = _cast_rows(D_MODEL, nb * ns)
        step_rows = lambda b, s: (b * ns + s, 0)
        next_rows = lambda b, s: (layer + 1, b * ns + s, 0)
        in_specs += [pl.BlockSpec((None, rows, IN_COLS), next_rows),
                     pl.BlockSpec((None, rows, d), next_rows)]
        operands += [w_in, w_out]
        out_specs += [pl.BlockSpec((rows, IN_COLS), step_rows),
                      pl.BlockSpec((rows, d), step_rows)]
        out_shape += [jax.ShapeDtypeStruct((D_MODEL, IN_COLS), BF16),
                      jax.ShapeDtypeStruct((D_MODEL, D_MODEL), BF16)]
    return pl.pallas_call(
        functools.partial(_mlp_kernel, layer=layer, final=final, cast_next=not final),
        grid=(nb, ns),
        in_specs=in_specs,
        out_specs=out_specs,
        out_shape=out_shape,
        compiler_params=pltpu.CompilerParams(
            dimension_semantics=("arbitrary", "arbitrary"),
            vmem_limit_bytes=VMEM_LIMIT_BYTES),
        name=f"mlp_{layer}",
    )(*operands)


def _tables(t):
    half = HEAD_DIM // 2
    inv_freq = ROPE_BASE ** (-np.arange(0, HEAD_DIM, 2, dtype=np.float64) / HEAD_DIM)
    ang = np.arange(t, dtype=np.float64)[:, None] * inv_freq[None, :]
    cos, sin = np.cos(ang), np.sin(ang)
    assert cos.shape == (t, half)
    cos2 = np.concatenate([cos, cos], axis=-1)
    sin2 = np.concatenate([-sin, sin], axis=-1)

    rc = RET_CHUNK
    k_scale = HEAD_DIM ** -0.5
    log_g = np.log1p(-np.exp2(-5.0 - np.arange(RET_HEADS, dtype=np.float64)))
    idx = np.arange(rc, dtype=np.float64)
    diff = idx[:, None] - idx[None, :]
    causal = diff >= 0
    dec = np.where(causal[None],
                   np.exp(log_g[:, None, None] * np.where(causal, diff, 0.0)[None]),
                   0.0) * k_scale
    xi = np.exp(log_g[:, None] * (idx + 1.0)[None])
    zeta = np.exp(log_g[:, None] * (rc - 1.0 - idx)[None]) * k_scale
    cd = np.exp(log_g * rc)
    lanes = (RET_HEADS, rc, HEAD_DIM)
    tables = (cos2, sin2, np.swapaxes(dec, 1, 2),
              np.broadcast_to(xi[:, :, None], lanes),
              np.broadcast_to(zeta[:, None, :], (RET_HEADS, 8, rc)),
              np.broadcast_to(cd[:, None, None], (RET_HEADS, 8, HEAD_DIM)))
    return tuple(jnp.asarray(a, dtype=F32) for a in tables)


def kernel(x, c, norm1_g, w_in, conv_w, ret_norm_g, w_out, norm2_g, w_up, w_down,
           w_ada, b_ada, final_g):
    nb, t, d = x.shape
    assert d == D_MODEL and t % MIX_TILE == 0 and t % MLP_TILE == 0
    assert MIX_TILE % RET_CHUNK == 0
    assert w_in.shape == (DEPTH, D_MODEL, IN_COLS)
    mod, w_in_b, w_out_b = _adaln(c, w_ada, b_ada, w_in, w_out)
    tables = _tables(t)
    fg = final_g.reshape(1, D_MODEL)
    for layer in range(DEPTH):
        x, w_up_b, w_down_b = _mixer(layer, x, mod, norm1_g, w_in_b, conv_w, ret_norm_g,
                                     w_out_b, w_up, w_down, tables)
        outs = _mlp(layer, x, mod, norm2_g, w_up_b, w_down_b, fg, w_in, w_out)
        if layer < DEPTH - 1:
            x, w_in_b, w_out_b = outs
        else:
            x, = outs
    return x
```

```python
import functools

import jax
import jax.numpy as jnp
import numpy as np
from jax import lax
from jax.experimental import pallas as pl
from jax.experimental.pallas import tpu as pltpu

D_MODEL = 1024
DEPTH = 4
CONV_WIDTH = 512
CONV_K = 3
RET_HEADS = 4
RET_WIDTH = 512
HEAD_DIM = 128
D_FF = 4 * D_MODEL
ROPE_BASE = 10000.0
EPS = 1e-6
N_MOD = 6
SEG = 512
IN_COLS = 7 * SEG

MIX_TILE = 1024
MLP_TILE = 1024
RET_CHUNK = 256
FF_CHUNK = 1024
ADA_COLS = 1536
HALO = 8
LOOKAHEAD = 2
HEAD_ROW_CHUNKS = 4
VMEM_LIMIT_BYTES = 56 * 1024 * 1024

F32 = jnp.float32
BF16 = jnp.bfloat16
BF16_SUBLANES = 16


def _resident(block_shape, index_map):
    return pl.BlockSpec(block_shape, index_map, pipeline_mode=pl.Buffered(1))


def _adaln_kernel(c_ref, w_ref, b_ref, w_in_f32_ref, w_out_f32_ref,
                  o_ref, w_in_bf16_ref, w_out_bf16_ref):
    w_in_bf16_ref[...] = w_in_f32_ref[...].astype(BF16)
    w_out_bf16_ref[...] = w_out_f32_ref[...].astype(BF16)

    c = c_ref[...]
    ca = c * jax.nn.sigmoid(c)
    ca_hi = ca.astype(BF16).astype(F32)
    lhs = jnp.concatenate([ca_hi, ca - ca_hi], axis=0).astype(BF16)
    w = w_ref[...]
    w_hi = w.astype(BF16)
    w_lo = (w - w_hi.astype(F32)).astype(BF16)
    acc = (jnp.dot(lhs, w_hi, preferred_element_type=F32)
           + jnp.dot(lhs, w_lo, preferred_element_type=F32))
    nb = c.shape[0]
    bias = b_ref[pl.ds(pl.program_id(0), 1), :]
    o_ref[...] = acc[0:nb] + acc[nb:2 * nb] + bias


def _adaln(c, w_ada, b_ada, w_in, w_out):
    nb = c.shape[0]
    n_col = N_MOD * D_MODEL // ADA_COLS
    rows = _cast_rows(D_MODEL, DEPTH * n_col)
    step_rows = lambda l, j: (l * n_col + j, 0)
    first_rows = lambda l, j: (0, l * n_col + j, 0)
    return pl.pallas_call(
        _adaln_kernel,
        grid=(DEPTH, n_col),
        in_specs=[
            pl.BlockSpec((nb, D_MODEL), lambda l, j: (0, 0)),
            pl.BlockSpec((None, D_MODEL, ADA_COLS), lambda l, j: (l, 0, j)),
            pl.BlockSpec((DEPTH, ADA_COLS), lambda l, j: (0, j)),
            pl.BlockSpec((None, rows, IN_COLS), first_rows),
            pl.BlockSpec((None, rows, D_MODEL), first_rows),
        ],
        out_specs=[
            pl.BlockSpec((None, nb, ADA_COLS), lambda l, j: (l, 0, j)),
            pl.BlockSpec((rows, IN_COLS), step_rows),
            pl.BlockSpec((rows, D_MODEL), step_rows),
        ],
        out_shape=[
            jax.ShapeDtypeStruct((DEPTH, nb, N_MOD * D_MODEL), F32),
            jax.ShapeDtypeStruct((D_MODEL, IN_COLS), BF16),
            jax.ShapeDtypeStruct((D_MODEL, D_MODEL), BF16),
        ],
        compiler_params=pltpu.CompilerParams(
            dimension_semantics=("arbitrary", "arbitrary"),
            vmem_limit_bytes=VMEM_LIMIT_BYTES),
        name="adaln_mod",
    )(c, w_ada, b_ada, w_in, w_out)


def _modulated_rms(x, gain, scale, shift):
    r = lax.rsqrt(jnp.mean(x * x, axis=-1, keepdims=True) + EPS)
    return ((x * r) * gain) * (1.0 + scale) + shift


def _mixer_kernel(x_ref, mod_ref, g_ref, w_in_ref, cw_ref, rg_ref, w_out_ref,
                  cos_ref, sin_ref, dec_ref, xi_ref, zeta_ref, cd_ref,
                  w_up_f32_ref, w_dn_f32_ref,
                  o_ref, w_up_bf16_ref, w_dn_bf16_ref, ubuf, ybuf, state, *, layer):
    ts = x_ref.shape[0]

    @pl.when(pl.program_id(1) == 0)
    def _start_of_sequence():
        ubuf[0:HALO, :] = jnp.zeros((HALO, CONV_WIDTH), F32)
        state[...] = jnp.zeros(state.shape, F32)

    x = x_ref[...]
    mod = mod_ref[pl.ds(pl.program_id(0), 1), :]
    shift = mod[:, 0:D_MODEL]
    scale = mod[:, D_MODEL:2 * D_MODEL]
    gate = mod[:, 2 * D_MODEL:3 * D_MODEL]
    hb = _modulated_rms(x, g_ref[pl.ds(layer, 1), :], scale, shift).astype(BF16)

    def proj(i):
        return jnp.dot(hb, w_in_ref[:, i * SEG:(i + 1) * SEG],
                       preferred_element_type=F32)

    q = proj(3)
    k = proj(4)
    v = proj(5)

    u = proj(1) * proj(2)
    ubuf[HALO:HALO + ts, :] = u
    u1 = ubuf[HALO - 1:HALO - 1 + ts, :]
    u2 = ubuf[HALO - 2:HALO - 2 + ts, :]
    cw = cw_ref[...]
    conv = u2 * cw[0:1, :] + u1 * cw[1:2, :] + u * cw[2:3, :]
    ubuf[0:HALO, :] = ubuf[ts:ts + HALO, :]

    cos2 = cos_ref[...]
    sin2 = sin_ref[...]
    rc = dec_ref.shape[1]
    n_chunks = ts // rc
    pairs = [(hh, ci) for hh in range(RET_HEADS) for ci in range(n_chunks)]
    head_cols = [slice(hh * HEAD_DIM, (hh + 1) * HEAD_DIM) for hh in range(RET_HEADS)]
    chunk_rows = [slice(ci * rc, (ci + 1) * rc) for ci in range(n_chunks)]
    qh, kh = [], []
    for cols in head_cols:
        qc, kc = q[:, cols], k[:, cols]
        qh.append(qc * cos2 + pltpu.roll(qc, HEAD_DIM // 2, 1) * sin2)
        kh.append(kc * cos2 + pltpu.roll(kc, HEAD_DIM // 2, 1) * sin2)

    gr = proj(6)

    scores_t, kv_t, v_t = {}, {}, {}
    s_now = [state[hh] for hh in range(RET_HEADS)]
    for step in range(len(pairs) + LOOKAHEAD):
        if step < len(pairs):
            hh, ci = pairs[step]
            rows = chunk_rows[ci]
            kc = kh[hh][rows].astype(BF16)
            scores_t[hh, ci] = lax.dot_general(
                kc, qh[hh][rows].astype(BF16),
                (((1,), (1,)), ((), ())), preferred_element_type=F32)
            vt = v[rows, head_cols[hh]].T
            v_t[hh, ci] = vt.astype(BF16)
            kv_t[hh, ci] = jnp.dot((vt * zeta_ref[hh][0:1, :]).astype(BF16), kc,
                                   preferred_element_type=F32)
        if step >= LOOKAHEAD:
            hh, ci = pairs[step - LOOKAHEAD]
            rows, cols = chunk_rows[ci], head_cols[hh]
            p_t = (scores_t.pop((hh, ci)) * dec_ref[hh]).astype(BF16)
            inner_t = jnp.dot(v_t.pop((hh, ci)), p_t, preferred_element_type=F32)
            cross_t = lax.dot_general(
                s_now[hh].astype(BF16), (qh[hh][rows] * xi_ref[hh]).astype(BF16),
                (((1,), (1,)), ((), ())), preferred_element_type=F32)
            s_now[hh] = s_now[hh] * cd_ref[hh][0:1, :] + kv_t.pop((hh, ci))
            o = (inner_t + cross_t).T
            mu = jnp.mean(o, axis=-1, keepdims=True)
            oc = o - mu
            var = jnp.mean(oc * oc, axis=-1, keepdims=True)
            on = (oc * lax.rsqrt(var + EPS)) * rg_ref[pl.ds(layer, 1), cols]
            g = gr[rows, cols]
            y = (g * jax.nn.sigmoid(g)) * on
            ybuf[rows, CONV_WIDTH + hh * HEAD_DIM:
                 CONV_WIDTH + (hh + 1) * HEAD_DIM] = y.astype(BF16)
    for hh in range(RET_HEADS):
        state[hh] = s_now[hh]

    cb = proj(0)
    ybuf[:, 0:CONV_WIDTH] = (cb * conv).astype(BF16)

    mix = (jnp.dot(ybuf[:, 0:CONV_WIDTH], w_out_ref[0:CONV_WIDTH, :],
                   preferred_element_type=F32)
           + jnp.dot(ybuf[:, CONV_WIDTH:], w_out_ref[CONV_WIDTH:, :],
                     preferred_element_type=F32))
    o_ref[...] = x + gate * mix

    w_up_bf16_ref[...] = w_up_f32_ref[...].astype(BF16)
    w_dn_bf16_ref[...] = w_dn_f32_ref[...].astype(BF16)


def _cast_rows(n_rows, n_steps):
    rows = n_rows // n_steps
    assert rows * n_steps == n_rows and rows % BF16_SUBLANES == 0
    return rows


def _mixer(layer, x, mod, norm_g, w_in, conv_w, ret_g, w_out, w_up, w_down, tables):
    nb, t, d = x.shape
    cos2, sin2, dec, xi, zeta, cd = tables
    rc = dec.shape[1]
    ts = MIX_TILE
    ns = t // ts
    up_rows, dn_rows = _cast_rows(D_MODEL, nb * ns), _cast_rows(D_FF, nb * ns)
    const2 = lambda b, s: (0, 0)
    const3 = lambda b, s: (0, 0, 0)
    lay2 = lambda b, s: (layer, 0, 0)
    return pl.pallas_call(
        functools.partial(_mixer_kernel, layer=layer),
        grid=(nb, ns),
        in_specs=[
            pl.BlockSpec((None, ts, d), lambda b, s: (b, s, 0)),
            _resident((None, nb, N_MOD * d), lay2),
            _resident((DEPTH, d), const2),
            _resident((d, IN_COLS), const2),
            _resident((None, CONV_K, CONV_WIDTH), lay2),
            _resident((DEPTH, RET_WIDTH), const2),
            _resident((d, d), const2),
            pl.BlockSpec((ts, HEAD_DIM), lambda b, s: (s, 0)),
            pl.BlockSpec((ts, HEAD_DIM), lambda b, s: (s, 0)),
            _resident((RET_HEADS, rc, rc), const3),
            _resident((RET_HEADS, rc, HEAD_DIM), const3),
            _resident((RET_HEADS, 8, rc), const3),
            _resident((RET_HEADS, 8, HEAD_DIM), const3),
            pl.BlockSpec((None, up_rows, D_FF), lambda b, s: (layer, b * ns + s, 0)),
            pl.BlockSpec((None, dn_rows, d), lambda b, s: (layer, b * ns + s, 0)),
        ],
        out_specs=[
            pl.BlockSpec((None, ts, d), lambda b, s: (b, s, 0)),
            pl.BlockSpec((up_rows, D_FF), lambda b, s: (b * ns + s, 0)),
            pl.BlockSpec((dn_rows, d), lambda b, s: (b * ns + s, 0)),
        ],
        out_shape=[
            jax.ShapeDtypeStruct(x.shape, F32),
            jax.ShapeDtypeStruct((D_MODEL, D_FF), BF16),
            jax.ShapeDtypeStruct((D_FF, D_MODEL), BF16),
        ],
        scratch_shapes=[
            pltpu.VMEM((ts + HALO, CONV_WIDTH), F32),
            pltpu.VMEM((ts, d), BF16),
            pltpu.VMEM((RET_HEADS, HEAD_DIM, HEAD_DIM), F32),
        ],
        compiler_params=pltpu.CompilerParams(
            dimension_semantics=("arbitrary", "arbitrary"),
            vmem_limit_bytes=VMEM_LIMIT_BYTES),
        name=f"mixer_{layer}",
    )(x, mod, norm_g, w_in, conv_w, ret_g, w_out, cos2, sin2, dec, xi, zeta, cd,
      w_up, w_down)


def _mlp_kernel(x_ref, mod_ref, g_ref, w_up_ref, w_dn_ref, fg_ref, *rest,
                layer, final, cast_next):
    o_ref = rest[2] if cast_next else rest[0]
    x = x_ref[...]
    mod = mod_ref[pl.ds(pl.program_id(0), 1), :]
    shift = mod[:, 3 * D_MODEL:4 * D_MODEL]
    scale = mod[:, 4 * D_MODEL:5 * D_MODEL]
    gate = mod[:, 5 * D_MODEL:6 * D_MODEL]
    gain = g_ref[pl.ds(layer, 1), :]
    ts = x.shape[0]
    row_chunks = [slice(i * (ts // HEAD_ROW_CHUNKS), (i + 1) * (ts // HEAD_ROW_CHUNKS))
                  for i in range(HEAD_ROW_CHUNKS)]
    hb_rows = [_modulated_rms(x[rows], gain, scale, shift).astype(BF16)
               for rows in row_chunks]
    hb = jnp.concatenate(hb_rows, axis=0)
    acc = None
    for ci in range(D_FF // FF_CHUNK):
        cols = slice(ci * FF_CHUNK, (ci + 1) * FF_CHUNK)
        if ci == 0:
            up = jnp.concatenate(
                [jnp.dot(h, w_up_ref[:, cols], preferred_element_type=F32)
                 for h in hb_rows], axis=0)
        else:
            up = jnp.dot(hb, w_up_ref[:, cols], preferred_element_type=F32)
        act = jnp.square(jnp.maximum(up, 0.0)).astype(BF16)
        if final and ci == D_FF // FF_CHUNK - 1:
            down = jnp.concatenate(
                [jnp.dot(act[rows], w_dn_ref[cols, :], preferred_element_type=F32)
                 for rows in row_chunks], axis=0)
        else:
            down = jnp.dot(act, w_dn_ref[cols, :], preferred_element_type=F32)
        acc = down if acc is None else acc + down
    y = x + gate * acc
    if final:
        r = lax.rsqrt(jnp.mean(y * y, axis=-1, keepdims=True) + EPS)
        y = (y * r) * fg_ref[...]
    o_ref[...] = y

    if cast_next:
        w_in_f32_ref, w_out_f32_ref, _, w_in_bf16_ref, w_out_bf16_ref = rest
        w_in_bf16_ref[...] = w_in_f32_ref[...].astype(BF16)
        w_out_bf16_ref[...] = w_out_f32_ref[...].astype(BF16)


def _mlp(layer, x, mod, norm_g, w_up, w_down, final_g, w_in, w_out):
    nb, t, d = x.shape
    ts = MLP_TILE
    ns = t // ts
    final = layer == DEPTH - 1
    const2 = lambda b, s: (0, 0)
    in_specs = [
        pl.BlockSpec((None, ts, d), lambda b, s: (b, s, 0)),
        _resident((None, nb, N_MOD * d), lambda b, s: (layer, 0, 0)),
        _resident((DEPTH, d), const2),
        _resident((d, D_FF), const2),
        _resident((D_FF, d), const2),
        _resident((1, d), const2),
    ]
    operands = [x, mod, norm_g, w_up, w_down, final_g]
    out_specs = [pl.BlockSpec((None, ts, d), lambda b, s: (b, s, 0))]
    out_shape = [jax.ShapeDtypeStruct(x.shape, F32)]
    if not final:
        rows = _cast_rows(D_MODEL, nb * ns)
        step_rows = lambda b, s: (b * ns + s, 0)
        next_rows = lambda b, s: (layer + 1, b * ns + s, 0)
        in_specs += [pl.BlockSpec((None, rows, IN_COLS), next_rows),
                     pl.BlockSpec((None, rows, d), next_rows)]
        operands += [w_in, w_out]
        out_specs += [pl.BlockSpec((rows, IN_COLS), step_rows),
                      pl.BlockSpec((rows, d), step_rows)]
        out_shape += [jax.ShapeDtypeStruct((D_MODEL, IN_COLS), BF16),
                      jax.ShapeDtypeStruct((D_MODEL, D_MODEL), BF16)]
    return pl.pallas_call(
        functools.partial(_mlp_kernel, layer=layer, final=final, cast_next=not final),
        grid=(nb, ns),
        in_specs=in_specs,
        out_specs=out_specs,
        out_shape=out_shape,
        compiler_params=pltpu.CompilerParams(
            dimension_semantics=("arbitrary", "arbitrary"),
            vmem_limit_bytes=VMEM_LIMIT_BYTES),
        name=f"mlp_{layer}",
    )(*operands)


def _tables(t):
    half = HEAD_DIM // 2
    inv_freq = ROPE_BASE ** (-np.arange(0, HEAD_DIM, 2, dtype=np.float64) / HEAD_DIM)
    ang = np.arange(t, dtype=np.float64)[:, None] * inv_freq[None, :]
    cos, sin = np.cos(ang), np.sin(ang)
    assert cos.shape == (t, half)
    cos2 = np.concatenate([cos, cos], axis=-1)
    sin2 = np.concatenate([-sin, sin], axis=-1)

    rc = RET_CHUNK
    k_scale = HEAD_DIM ** -0.5
    log_g = np.log1p(-np.exp2(-5.0 - np.arange(RET_HEADS, dtype=np.float64)))
    idx = np.arange(rc, dtype=np.float64)
    diff = idx[:, None] - idx[None, :]
    causal = diff >= 0
    dec = np.where(causal[None],
                   np.exp(log_g[:, None, None] * np.where(causal, diff, 0.0)[None]),
                   0.0) * k_scale
    xi = np.exp(log_g[:, None] * (idx + 1.0)[None])
    zeta = np.exp(log_g[:, None] * (rc - 1.0 - idx)[None]) * k_scale
    cd = np.exp(log_g * rc)
    lanes = (RET_HEADS, rc, HEAD_DIM)
    tables = (cos2, sin2, np.swapaxes(dec, 1, 2),
              np.broadcast_to(xi[:, :, None], lanes),
              np.broadcast_to(zeta[:, None, :], (RET_HEADS, 8, rc)),
              np.broadcast_to(cd[:, None, None], (RET_HEADS, 8, HEAD_DIM)))
    return tuple(jnp.asarray(a, dtype=F32) for a in tables)


def kernel(x, c, norm1_g, w_in, conv_w, ret_norm_g, w_out, norm2_g, w_up, w_down,
           w_ada, b_ada, final_g):
    nb, t, d = x.shape
    assert d == D_MODEL and t % MIX_TILE == 0 and t % MLP_TILE == 0
    assert MIX_TILE % RET_CHUNK == 0
    assert w_in.shape == (DEPTH, D_MODEL, IN_COLS)
    mod, w_in_b, w_out_b = _adaln(c, w_ada, b_ada, w_in, w_out)
    tables = _tables(t)
    fg = final_g.reshape(1, D_MODEL)
    for layer in range(DEPTH):
        x, w_up_b, w_down_b = _mixer(layer, x, mod, norm1_g, w_in_b, conv_w, ret_norm_g,
                                     w_out_b, w_up, w_down, tables)
        outs = _mlp(layer, x, mod, norm2_g, w_up_b, w_down_b, fg, w_in, w_out)
        if layer < DEPTH - 1:
            x, w_in_b, w_out_b = outs
        else:
            x, = outs
    return x
```

```python
import functools

import jax
import jax.numpy as jnp
import numpy as np
from jax import lax
from jax.experimental import pallas as pl
from jax.experimental.pallas import tpu as pltpu

D_MODEL = 1024
DEPTH = 4
CONV_WIDTH = 512
CONV_K = 3
RET_HEADS = 4
RET_WIDTH = 512
HEAD_DIM = 128
D_FF = 4 * D_MODEL
ROPE_BASE = 10000.0
EPS = 1e-6
N_MOD = 6
SEG = 512
IN_COLS = 7 * SEG

MIX_TILE = 1024
MLP_TILE = 1024
RET_CHUNK = 256
FF_CHUNK = 1024
ADA_COLS = 3072
LOOKAHEAD = 2
HEAD_ROW_CHUNKS = 4
VMEM_LIMIT_BYTES = 56 * 1024 * 1024

F32 = jnp.float32
BF16 = jnp.bfloat16
F32_SUBLANES = 8
BF16_SUBLANES = 16
HALO = F32_SUBLANES


def _resident(block_shape, index_map):
    return pl.BlockSpec(block_shape, index_map, pipeline_mode=pl.Buffered(1))


def _adaln_kernel(c_ref, w_ref, b_ref, w_in_f32_ref, w_out_f32_ref,
                  o_ref, w_in_bf16_ref, w_out_bf16_ref):
    w_in_bf16_ref[...] = w_in_f32_ref[...].astype(BF16)
    w_out_bf16_ref[...] = w_out_f32_ref[...].astype(BF16)

    c = c_ref[...]
    ca = c * jax.nn.sigmoid(c)
    ca_hi = ca.astype(BF16).astype(F32)
    lhs = jnp.concatenate([ca_hi, ca - ca_hi], axis=0).astype(BF16)
    w = w_ref[...]
    w_hi = w.astype(BF16)
    w_lo = (w - w_hi.astype(F32)).astype(BF16)
    acc = (jnp.dot(lhs, w_hi, preferred_element_type=F32)
           + jnp.dot(lhs, w_lo, preferred_element_type=F32))
    nb = c.shape[0]
    bias = b_ref[pl.ds(pl.program_id(0), 1), :]
    o_ref[...] = acc[0:nb] + acc[nb:2 * nb] + bias


def _adaln(c, w_ada, b_ada, w_in, w_out):
    nb = c.shape[0]
    n_col = N_MOD * D_MODEL // ADA_COLS
    rows = _cast_rows(D_MODEL, DEPTH * n_col)
    step_rows = lambda l, j: (l * n_col + j, 0)
    first_rows = lambda l, j: (0, l * n_col + j, 0)
    return pl.pallas_call(
        _adaln_kernel,
        grid=(DEPTH, n_col),
        in_specs=[
            pl.BlockSpec((nb, D_MODEL), lambda l, j: (0, 0)),
            pl.BlockSpec((None, D_MODEL, ADA_COLS), lambda l, j: (l, 0, j)),
            pl.BlockSpec((DEPTH, ADA_COLS), lambda l, j: (0, j)),
            pl.BlockSpec((None, rows, IN_COLS), first_rows),
            pl.BlockSpec((None, rows, D_MODEL), first_rows),
        ],
        out_specs=[
            pl.BlockSpec((None, nb, ADA_COLS), lambda l, j: (l, 0, j)),
            pl.BlockSpec((rows, IN_COLS), step_rows),
            pl.BlockSpec((rows, D_MODEL), step_rows),
        ],
        out_shape=[
            jax.ShapeDtypeStruct((DEPTH, nb, N_MOD * D_MODEL), F32),
            jax.ShapeDtypeStruct((D_MODEL, IN_COLS), BF16),
            jax.ShapeDtypeStruct((D_MODEL, D_MODEL), BF16),
        ],
        compiler_params=pltpu.CompilerParams(
            dimension_semantics=("arbitrary", "arbitrary"),
            vmem_limit_bytes=VMEM_LIMIT_BYTES),
        name="adaln_mod",
    )(c, w_ada, b_ada, w_in, w_out)


def _modulated_rms(x, gain, scale, shift):
    r = lax.rsqrt(jnp.mean(x * x, axis=-1, keepdims=True) + EPS)
    return ((x * r) * gain) * (1.0 + scale) + shift


def _mixer_kernel(x_ref, mod_ref, g_ref, w_in_ref, cw_ref, rg_ref, w_out_ref,
                  cos_ref, sin_ref, dec_ref, xi_ref, zeta_ref, cd_ref,
                  w_up_f32_ref, w_dn_f32_ref,
                  o_ref, w_up_bf16_ref, w_dn_bf16_ref, ubuf, ybuf, state, *, layer):
    ts = x_ref.shape[0]

    @pl.when(pl.program_id(1) == 0)
    def _start_of_sequence():
        ubuf[0:HALO, :] = jnp.zeros((HALO, CONV_WIDTH), F32)
        state[...] = jnp.zeros(state.shape, F32)

    x = x_ref[...]
    mod = mod_ref[pl.ds(pl.program_id(0), 1), :]
    shift = mod[:, 0:D_MODEL]
    scale = mod[:, D_MODEL:2 * D_MODEL]
    gate = mod[:, 2 * D_MODEL:3 * D_MODEL]
    hb = _modulated_rms(x, g_ref[pl.ds(layer, 1), :], scale, shift).astype(BF16)

    def proj(i):
        return jnp.dot(hb, w_in_ref[:, i * SEG:(i + 1) * SEG],
                       preferred_element_type=F32)

    q = proj(3)
    k = proj(4)
    v = proj(5)

    u = proj(1) * proj(2)
    ubuf[HALO:HALO + ts, :] = u
    u1 = ubuf[HALO - 1:HALO - 1 + ts, :]
    u2 = ubuf[HALO - 2:HALO - 2 + ts, :]
    cw = cw_ref[...]
    conv = u2 * cw[0:1, :] + u1 * cw[1:2, :] + u * cw[2:3, :]
    ubuf[0:HALO, :] = ubuf[ts:ts + HALO, :]

    cos2 = cos_ref[...]
    sin2 = sin_ref[...]
    rc = dec_ref.shape[1]
    n_chunks = ts // rc
    pairs = [(hh, ci) for hh in range(RET_HEADS) for ci in range(n_chunks)]
    head_cols = [slice(hh * HEAD_DIM, (hh + 1) * HEAD_DIM) for hh in range(RET_HEADS)]
    chunk_rows = [slice(ci * rc, (ci + 1) * rc) for ci in range(n_chunks)]
    qh, kh = [], []
    for cols in head_cols:
        qc, kc = q[:, cols], k[:, cols]
        qh.append(qc * cos2 + pltpu.roll(qc, HEAD_DIM // 2, 1) * sin2)
        kh.append(kc * cos2 + pltpu.roll(kc, HEAD_DIM // 2, 1) * sin2)

    gr = proj(6)

    scores_t, kv_t, v_t = {}, {}, {}
    s_now = [state[hh] for hh in range(RET_HEADS)]
    for step in range(len(pairs) + LOOKAHEAD):
        if step < len(pairs):
            hh, ci = pairs[step]
            rows = chunk_rows[ci]
            kc = kh[hh][rows].astype(BF16)
            scores_t[hh, ci] = lax.dot_general(
                kc, qh[hh][rows].astype(BF16),
                (((1,), (1,)), ((), ())), preferred_element_type=F32)
            vt = v[rows, head_cols[hh]].T
            v_t[hh, ci] = vt.astype(BF16)
            kv_t[hh, ci] = jnp.dot((vt * zeta_ref[hh][0:1, :]).astype(BF16), kc,
                                   preferred_element_type=F32)
        if step >= LOOKAHEAD:
            hh, ci = pairs[step - LOOKAHEAD]
            rows, cols = chunk_rows[ci], head_cols[hh]
            p_t = (scores_t.pop((hh, ci)) * dec_ref[hh]).astype(BF16)
            inner_t = jnp.dot(v_t.pop((hh, ci)), p_t, preferred_element_type=F32)
            cross_t = lax.dot_general(
                s_now[hh].astype(BF16), (qh[hh][rows] * xi_ref[hh]).astype(BF16),
                (((1,), (1,)), ((), ())), preferred_element_type=F32)
            s_now[hh] = s_now[hh] * cd_ref[hh][0:1, :] + kv_t.pop((hh, ci))
            o = (inner_t + cross_t).T
            mu = jnp.mean(o, axis=-1, keepdims=True)
            oc = o - mu
            var = jnp.mean(oc * oc, axis=-1, keepdims=True)
            on = (oc * lax.rsqrt(var + EPS)) * rg_ref[pl.ds(layer, 1), cols]
            g = gr[rows, cols]
            y = (g * jax.nn.sigmoid(g)) * on
            ybuf[rows, CONV_WIDTH + hh * HEAD_DIM:
                 CONV_WIDTH + (hh + 1) * HEAD_DIM] = y.astype(BF16)
    for hh in range(RET_HEADS):
        state[hh] = s_now[hh]

    cb = proj(0)
    ybuf[:, 0:CONV_WIDTH] = (cb * conv).astype(BF16)

    mix = (jnp.dot(ybuf[:, 0:CONV_WIDTH], w_out_ref[0:CONV_WIDTH, :],
                   preferred_element_type=F32)
           + jnp.dot(ybuf[:, CONV_WIDTH:], w_out_ref[CONV_WIDTH:, :],
                     preferred_element_type=F32))
    o_ref[...] = x + gate * mix

    w_up_bf16_ref[...] = w_up_f32_ref[...].astype(BF16)
    w_dn_bf16_ref[...] = w_dn_f32_ref[...].astype(BF16)


def _cast_rows(n_rows, n_steps):
    rows = n_rows // n_steps
    assert rows * n_steps == n_rows and rows % BF16_SUBLANES == 0
    return rows


def _mixer(layer, x, mod, norm_g, w_in, conv_w, ret_g, w_out, w_up, w_down, tables):
    nb, t, d = x.shape
    cos2, sin2, dec, xi, zeta, cd = tables
    rc = dec.shape[1]
    ts = MIX_TILE
    ns = t // ts
    up_rows, dn_rows = _cast_rows(D_MODEL, nb * ns), _cast_rows(D_FF, nb * ns)
    const2 = lambda b, s: (0, 0)
    const3 = lambda b, s: (0, 0, 0)
    lay2 = lambda b, s: (layer, 0, 0)
    return pl.pallas_call(
        functools.partial(_mixer_kernel, layer=layer),
        grid=(nb, ns),
        in_specs=[
            pl.BlockSpec((None, ts, d), lambda b, s: (b, s, 0)),
            _resident((None, nb, N_MOD * d), lay2),
            _resident((DEPTH, d), const2),
            _resident((d, IN_COLS), const2),
            _resident((None, CONV_K, CONV_WIDTH), lay2),
            _resident((DEPTH, RET_WIDTH), const2),
            _resident((d, d), const2),
            pl.BlockSpec((ts, HEAD_DIM), lambda b, s: (s, 0)),
            pl.BlockSpec((ts, HEAD_DIM), lambda b, s: (s, 0)),
            _resident((RET_HEADS, rc, rc), const3),
            _resident((RET_HEADS, rc, HEAD_DIM), const3),
            _resident((RET_HEADS, F32_SUBLANES, rc), const3),
            _resident((RET_HEADS, F32_SUBLANES, HEAD_DIM), const3),
            pl.BlockSpec((None, up_rows, D_FF), lambda b, s: (layer, b * ns + s, 0)),
            pl.BlockSpec((None, dn_rows, d), lambda b, s: (layer, b * ns + s, 0)),
        ],
        out_specs=[
            pl.BlockSpec((None, ts, d), lambda b, s: (b, s, 0)),
            pl.BlockSpec((up_rows, D_FF), lambda b, s: (b * ns + s, 0)),
            pl.BlockSpec((dn_rows, d), lambda b, s: (b * ns + s, 0)),
        ],
        out_shape=[
            jax.ShapeDtypeStruct(x.shape, F32),
            jax.ShapeDtypeStruct((D_MODEL, D_FF), BF16),
            jax.ShapeDtypeStruct((D_FF, D_MODEL), BF16),
        ],
        scratch_shapes=[
            pltpu.VMEM((ts + HALO, CONV_WIDTH), F32),
            pltpu.VMEM((ts, d), BF16),
            pltpu.VMEM((RET_HEADS, HEAD_DIM, HEAD_DIM), F32),
        ],
        compiler_params=pltpu.CompilerParams(
            dimension_semantics=("arbitrary", "arbitrary"),
            vmem_limit_bytes=VMEM_LIMIT_BYTES),
        name=f"mixer_{layer}",
    )(x, mod, norm_g, w_in, conv_w, ret_g, w_out, cos2, sin2, dec, xi, zeta, cd,
      w_up, w_down)


def _mlp_kernel(x_ref, mod_ref, g_ref, w_up_ref, w_dn_ref, fg_ref, *rest,
                layer, final, cast_next):
    o_ref = rest[2] if cast_next else rest[0]
    x = x_ref[...]
    mod = mod_ref[pl.ds(pl.program_id(0), 1), :]
    shift = mod[:, 3 * D_MODEL:4 * D_MODEL]
    scale = mod[:, 4 * D_MODEL:5 * D_MODEL]
    gate = mod[:, 5 * D_MODEL:6 * D_MODEL]
    gain = g_ref[pl.ds(layer, 1), :]
    ts = x.shape[0]
    row_chunks = [slice(i * (ts // HEAD_ROW_CHUNKS), (i + 1) * (ts // HEAD_ROW_CHUNKS))
                  for i in range(HEAD_ROW_CHUNKS)]
    hb_rows = [_modulated_rms(x[rows], gain, scale, shift).astype(BF16)
               for rows in row_chunks]
    hb = jnp.concatenate(hb_rows, axis=0)
    acc = None
    for ci in range(D_FF // FF_CHUNK):
        cols = slice(ci * FF_CHUNK, (ci + 1) * FF_CHUNK)
        if ci == 0:
            up = jnp.concatenate(
                [jnp.dot(h, w_up_ref[:, cols], preferred_element_type=F32)
                 for h in hb_rows], axis=0)
        else:
            up = jnp.dot(hb, w_up_ref[:, cols], preferred_element_type=F32)
        act = jnp.square(jnp.maximum(up, 0.0)).astype(BF16)
        if final and ci == D_FF // FF_CHUNK - 1:
            down = jnp.concatenate(
                [jnp.dot(act[rows], w_dn_ref[cols, :], preferred_element_type=F32)
                 for rows in row_chunks], axis=0)
        else:
            down = jnp.dot(act, w_dn_ref[cols, :], preferred_element_type=F32)
        acc = down if acc is None else acc + down
    y = x + gate * acc
    if final:
        r = lax.rsqrt(jnp.mean(y * y, axis=-1, keepdims=True) + EPS)
        y = (y * r) * fg_ref[...]
    o_ref[...] = y

    if cast_next:
        w_in_f32_ref, w_out_f32_ref, _, w_in_bf16_ref, w_out_bf16_ref = rest
        w_in_bf16_ref[...] = w_in_f32_ref[...].astype(BF16)
        w_out_bf16_ref[...] = w_out_f32_ref[...].astype(BF16)


def _mlp(layer, x, mod, norm_g, w_up, w_down, final_g, w_in, w_out):
    nb, t, d = x.shape
    ts = MLP_TILE
    ns = t // ts
    final = layer == DEPTH - 1
    const2 = lambda b, s: (0, 0)
    in_specs = [
        pl.BlockSpec((None, ts, d), lambda b, s: (b, s, 0)),
        _resident((None, nb, N_MOD * d), lambda b, s: (layer, 0, 0)),
        _resident((DEPTH, d), const2),
        _resident((d, D_FF), const2),
        _resident((D_FF, d), const2),
        _resident((1, d), const2),
    ]
    operands = [x, mod, norm_g, w_up, w_down, final_g]
    out_specs = [pl.BlockSpec((None, ts, d), lambda b, s: (b, s, 0))]
    out_shape = [jax.ShapeDtypeStruct(x.shape, F32)]
    if not final:
        rows = _cast_rows(D_MODEL, nb * ns)
        step_rows = lambda b, s: (b * ns + s, 0)
        next_rows = lambda b, s: (layer + 1, b * ns + s, 0)
        in_specs += [pl.BlockSpec((None, rows, IN_COLS), next_rows),
                     pl.BlockSpec((None, rows, d), next_rows)]
        operands += [w_in, w_out]
        out_specs += [pl.BlockSpec((rows, IN_COLS), step_rows),
                      pl.BlockSpec((rows, d), step_rows)]
        out_shape += [jax.ShapeDtypeStruct((D_MODEL, IN_COLS), BF16),
                      jax.ShapeDtypeStruct((D_MODEL, D_MODEL), BF16)]
    return pl.pallas_call(
        functools.partial(_mlp_kernel, layer=layer, final=final, cast_next=not final),
        grid=(nb, ns),
        in_specs=in_specs,
        out_specs=out_specs,
        out_shape=out_shape,
        compiler_params=pltpu.CompilerParams(
            dimension_semantics=("arbitrary", "arbitrary"),
            vmem_limit_bytes=VMEM_LIMIT_BYTES),
        name=f"mlp_{layer}",
    )(*operands)


def _tables(t):
    half = HEAD_DIM // 2
    inv_freq = ROPE_BASE ** (-np.arange(0, HEAD_DIM, 2, dtype=np.float64) / HEAD_DIM)
    ang = np.arange(t, dtype=np.float64)[:, None] * inv_freq[None, :]
    cos, sin = np.cos(ang), np.sin(ang)
    assert cos.shape == (t, half)
    cos2 = np.concatenate([cos, cos], axis=-1)
    sin2 = np.concatenate([-sin, sin], axis=-1)

    rc = RET_CHUNK
    k_scale = HEAD_DIM ** -0.5
    log_g = np.log1p(-np.exp2(-5.0 - np.arange(RET_HEADS, dtype=np.float64)))
    idx = np.arange(rc, dtype=np.float64)
    diff = idx[:, None] - idx[None, :]
    causal = diff >= 0
    dec = np.where(causal[None],
                   np.exp(log_g[:, None, None] * np.where(causal, diff, 0.0)[None]),
                   0.0) * k_scale
    xi = np.exp(log_g[:, None] * (idx + 1.0)[None])
    zeta = np.exp(log_g[:, None] * (rc - 1.0 - idx)[None]) * k_scale
    cd = np.exp(log_g * rc)
    lanes = (RET_HEADS, rc, HEAD_DIM)
    tables = (cos2, sin2, np.swapaxes(dec, 1, 2),
              np.broadcast_to(xi[:, :, None], lanes),
              np.broadcast_to(zeta[:, None, :], (RET_HEADS, F32_SUBLANES, rc)),
              np.broadcast_to(cd[:, None, None], (RET_HEADS, F32_SUBLANES, HEAD_DIM)))
    return tuple(jnp.asarray(a, dtype=F32) for a in tables)


def kernel(x, c, norm1_g, w_in, conv_w, ret_norm_g, w_out, norm2_g, w_up, w_down,
           w_ada, b_ada, final_g):
    nb, t, d = x.shape
    assert d == D_MODEL and t % MIX_TILE == 0 and t % MLP_TILE == 0
    assert MIX_TILE % RET_CHUNK == 0
    assert w_in.shape == (DEPTH, D_MODEL, IN_COLS)
    mod, w_in_b, w_out_b = _adaln(c, w_ada, b_ada, w_in, w_out)
    tables = _tables(t)
    fg = final_g.reshape(1, D_MODEL)
    for layer in range(DEPTH):
        x, w_up_b, w_down_b = _mixer(layer, x, mod, norm1_g, w_in_b, conv_w, ret_norm_g,
                                     w_out_b, w_up, w_down, tables)
        outs = _mlp(layer, x, mod, norm2_g, w_up_b, w_down_b, fg, w_in, w_out)
        if layer < DEPTH - 1:
            x, w_in_b, w_out_b = outs
        else:
            x, = outs
    return x
```

```python
import functools

import jax
import jax.numpy as jnp
import numpy as np
from jax import lax
from jax.experimental import pallas as pl
from jax.experimental.pallas import tpu as pltpu

D_MODEL = 1024
DEPTH = 4
CONV_WIDTH = 512
CONV_K = 3
RET_HEADS = 4
RET_WIDTH = 512
HEAD_DIM = 128
D_FF = 4 * D_MODEL
ROPE_BASE = 10000.0
EPS = 1e-6
N_MOD = 6
SEG = 512
IN_COLS = 7 * SEG

MIX_TILE = 1024
MLP_TILE = 1024
RET_CHUNK = 256
FF_CHUNK = 1024
ADA_COLS = 1536
LOOKAHEAD = 2
HEAD_ROW_CHUNKS = 4
MIX_HEAD_ROW_CHUNKS = 2
VMEM_LIMIT_BYTES = 56 * 1024 * 1024

F32 = jnp.float32
BF16 = jnp.bfloat16
F32_SUBLANES = 8
BF16_SUBLANES = 16
HALO = F32_SUBLANES


def _resident(block_shape, index_map):
    return pl.BlockSpec(block_shape, index_map, pipeline_mode=pl.Buffered(1))


def _adaln_kernel(c_ref, w_ref, b_ref, w_in_f32_ref, w_out_f32_ref,
                  o_ref, w_in_bf16_ref, w_out_bf16_ref):
    w_in_bf16_ref[...] = w_in_f32_ref[...].astype(BF16)
    w_out_bf16_ref[...] = w_out_f32_ref[...].astype(BF16)

    c = c_ref[...]
    ca = c * jax.nn.sigmoid(c)
    ca_hi = ca.astype(BF16).astype(F32)
    lhs = jnp.concatenate([ca_hi, ca - ca_hi], axis=0).astype(BF16)
    w = w_ref[...]
    w_hi = w.astype(BF16)
    w_lo = (w - w_hi.astype(F32)).astype(BF16)
    acc = (jnp.dot(lhs, w_hi, preferred_element_type=F32)
           + jnp.dot(lhs, w_lo, preferred_element_type=F32))
    nb = c.shape[0]
    bias = b_ref[pl.ds(pl.program_id(0), 1), :]
    o_ref[...] = acc[0:nb] + acc[nb:2 * nb] + bias


def _adaln(c, w_ada, b_ada, w_in, w_out):
    nb = c.shape[0]
    n_col = N_MOD * D_MODEL // ADA_COLS
    rows = _cast_rows(D_MODEL, DEPTH * n_col)
    step_rows = lambda l, j: (l * n_col + j, 0)
    first_rows = lambda l, j: (0, l * n_col + j, 0)
    return pl.pallas_call(
        _adaln_kernel,
        grid=(DEPTH, n_col),
        in_specs=[
            pl.BlockSpec((nb, D_MODEL), lambda l, j: (0, 0)),
            pl.BlockSpec((None, D_MODEL, ADA_COLS), lambda l, j: (l, 0, j)),
            pl.BlockSpec((DEPTH, ADA_COLS), lambda l, j: (0, j)),
            pl.BlockSpec((None, rows, IN_COLS), first_rows),
            pl.BlockSpec((None, rows, D_MODEL), first_rows),
        ],
        out_specs=[
            pl.BlockSpec((None, nb, ADA_COLS), lambda l, j: (l, 0, j)),
            pl.BlockSpec((rows, IN_COLS), step_rows),
            pl.BlockSpec((rows, D_MODEL), step_rows),
        ],
        out_shape=[
            jax.ShapeDtypeStruct((DEPTH, nb, N_MOD * D_MODEL), F32),
            jax.ShapeDtypeStruct((D_MODEL, IN_COLS), BF16),
            jax.ShapeDtypeStruct((D_MODEL, D_MODEL), BF16),
        ],
        compiler_params=pltpu.CompilerParams(
            dimension_semantics=("arbitrary", "arbitrary"),
            vmem_limit_bytes=VMEM_LIMIT_BYTES),
        name="adaln_mod",
    )(c, w_ada, b_ada, w_in, w_out)


def _modulated_rms(x, gain, scale, shift):
    r = lax.rsqrt(jnp.mean(x * x, axis=-1, keepdims=True) + EPS)
    return ((x * r) * gain) * (1.0 + scale) + shift


def _mixer_kernel(x_ref, mod_ref, g_ref, w_in_ref, cw_ref, rg_ref, w_out_ref,
                  cos_ref, sin_ref, dec_ref, xi_ref, zeta_ref, cd_ref,
                  w_up_f32_ref, w_dn_f32_ref,
                  o_ref, w_up_bf16_ref, w_dn_bf16_ref, ubuf, ybuf, state, *, layer):
    ts = x_ref.shape[0]

    @pl.when(pl.program_id(1) == 0)
    def _start_of_sequence():
        ubuf[0:HALO, :] = jnp.zeros((HALO, CONV_WIDTH), F32)
        state[...] = jnp.zeros(state.shape, F32)

    x = x_ref[...]
    mod = mod_ref[pl.ds(pl.program_id(0), 1), :]
    shift = mod[:, 0:D_MODEL]
    scale = mod[:, D_MODEL:2 * D_MODEL]
    gate = mod[:, 2 * D_MODEL:3 * D_MODEL]
    gain = g_ref[pl.ds(layer, 1), :]
    n_head = MIX_HEAD_ROW_CHUNKS
    hb_rows = [_modulated_rms(x[i * (ts // n_head):(i + 1) * (ts // n_head)],
                              gain, scale, shift).astype(BF16) for i in range(n_head)]
    hb = jnp.concatenate(hb_rows, axis=0)

    def proj(i):
        return jnp.dot(hb, w_in_ref[:, i * SEG:(i + 1) * SEG],
                       preferred_element_type=F32)

    q = jnp.concatenate(
        [jnp.dot(h, w_in_ref[:, 3 * SEG:4 * SEG], preferred_element_type=F32)
         for h in hb_rows], axis=0)
    k = proj(4)
    v = proj(5)

    u = proj(1) * proj(2)
    ubuf[HALO:HALO + ts, :] = u
    u1 = ubuf[HALO - 1:HALO - 1 + ts, :]
    u2 = ubuf[HALO - 2:HALO - 2 + ts, :]
    cw = cw_ref[...]
    conv = u2 * cw[0:1, :] + u1 * cw[1:2, :] + u * cw[2:3, :]
    ubuf[0:HALO, :] = ubuf[ts:ts + HALO, :]

    cos2 = cos_ref[...]
    sin2 = sin_ref[...]
    rc = dec_ref.shape[1]
    n_chunks = ts // rc
    pairs = [(hh, ci) for hh in range(RET_HEADS) for ci in range(n_chunks)]
    head_cols = [slice(hh * HEAD_DIM, (hh + 1) * HEAD_DIM) for hh in range(RET_HEADS)]
    chunk_rows = [slice(ci * rc, (ci + 1) * rc) for ci in range(n_chunks)]
    qh, kh = [], []
    for cols in head_cols:
        qc, kc = q[:, cols], k[:, cols]
        qh.append(qc * cos2 + pltpu.roll(qc, HEAD_DIM // 2, 1) * sin2)
        kh.append(kc * cos2 + pltpu.roll(kc, HEAD_DIM // 2, 1) * sin2)

    gr = proj(6)

    scores_t, kv_t, v_t = {}, {}, {}
    s_now = [state[hh] for hh in range(RET_HEADS)]
    for step in range(len(pairs) + LOOKAHEAD):
        if step < len(pairs):
            hh, ci = pairs[step]
            rows = chunk_rows[ci]
            kc = kh[hh][rows].astype(BF16)
            scores_t[hh, ci] = lax.dot_general(
                kc, qh[hh][rows].astype(BF16),
                (((1,), (1,)), ((), ())), preferred_element_type=F32)
            vt = v[rows, head_cols[hh]].T
            v_t[hh, ci] = vt.astype(BF16)
            kv_t[hh, ci] = jnp.dot((vt * zeta_ref[hh][0:1, :]).astype(BF16), kc,
                                   preferred_element_type=F32)
        if step >= LOOKAHEAD:
            hh, ci = pairs[step - LOOKAHEAD]
            rows, cols = chunk_rows[ci], head_cols[hh]
            p_t = (scores_t.pop((hh, ci)) * dec_ref[hh]).astype(BF16)
            inner_t = jnp.dot(v_t.pop((hh, ci)), p_t, preferred_element_type=F32)
            cross_t = lax.dot_general(
                s_now[hh].astype(BF16), (qh[hh][rows] * xi_ref[hh]).astype(BF16),
                (((1,), (1,)), ((), ())), preferred_element_type=F32)
            s_now[hh] = s_now[hh] * cd_ref[hh][0:1, :] + kv_t.pop((hh, ci))
            o = (inner_t + cross_t).T
            mu = jnp.mean(o, axis=-1, keepdims=True)
            oc = o - mu
            var = jnp.mean(oc * oc, axis=-1, keepdims=True)
            on = (oc * lax.rsqrt(var + EPS)) * rg_ref[pl.ds(layer, 1), cols]
            g = gr[rows, cols]
            y = (g * jax.nn.sigmoid(g)) * on
            ybuf[rows, CONV_WIDTH + hh * HEAD_DIM:
                 CONV_WIDTH + (hh + 1) * HEAD_DIM] = y.astype(BF16)
    for hh in range(RET_HEADS):
        state[hh] = s_now[hh]

    cb = proj(0)
    ybuf[:, 0:CONV_WIDTH] = (cb * conv).astype(BF16)

    mix = (jnp.dot(ybuf[:, 0:CONV_WIDTH], w_out_ref[0:CONV_WIDTH, :],
                   preferred_element_type=F32)
           + jnp.dot(ybuf[:, CONV_WIDTH:], w_out_ref[CONV_WIDTH:, :],
                     preferred_element_type=F32))
    o_ref[...] = x + gate * mix

    w_up_bf16_ref[...] = w_up_f32_ref[...].astype(BF16)
    w_dn_bf16_ref[...] = w_dn_f32_ref[...].astype(BF16)


def _cast_rows(n_rows, n_steps):
    rows = n_rows // n_steps
    assert rows * n_steps == n_rows and rows % BF16_SUBLANES == 0
    return rows


def _mixer(layer, x, mod, norm_g, w_in, conv_w, ret_g, w_out, w_up, w_down, tables):
    nb, t, d = x.shape
    cos2, sin2, dec, xi, zeta, cd = tables
    rc = dec.shape[1]
    ts = MIX_TILE
    ns = t // ts
    up_rows, dn_rows = _cast_rows(D_MODEL, nb * ns), _cast_rows(D_FF, nb * ns)
    const2 = lambda b, s: (0, 0)
    const3 = lambda b, s: (0, 0, 0)
    lay2 = lambda b, s: (layer, 0, 0)
    return pl.pallas_call(
        functools.partial(_mixer_kernel, layer=layer),
        grid=(nb, ns),
        in_specs=[
            pl.BlockSpec((None, ts, d), lambda b, s: (b, s, 0)),
            _resident((None, nb, N_MOD * d), lay2),
            _resident((DEPTH, d), const2),
            _resident((d, IN_COLS), const2),
            _resident((None, CONV_K, CONV_WIDTH), lay2),
            _resident((DEPTH, RET_WIDTH), const2),
            _resident((d, d), const2),
            pl.BlockSpec((ts, HEAD_DIM), lambda b, s: (s, 0)),
            pl.BlockSpec((ts, HEAD_DIM), lambda b, s: (s, 0)),
            _resident((RET_HEADS, rc, rc), const3),
            _resident((RET_HEADS, rc, HEAD_DIM), const3),
            _resident((RET_HEADS, F32_SUBLANES, rc), const3),
            _resident((RET_HEADS, F32_SUBLANES, HEAD_DIM), const3),
            pl.BlockSpec((None, up_rows, D_FF), lambda b, s: (layer, b * ns + s, 0)),
            pl.BlockSpec((None, dn_rows, d), lambda b, s: (layer, b * ns + s, 0)),
        ],
        out_specs=[
            pl.BlockSpec((None, ts, d), lambda b, s: (b, s, 0)),
            pl.BlockSpec((up_rows, D_FF), lambda b, s: (b * ns + s, 0)),
            pl.BlockSpec((dn_rows, d), lambda b, s: (b * ns + s, 0)),
        ],
        out_shape=[
            jax.ShapeDtypeStruct(x.shape, F32),
            jax.ShapeDtypeStruct((D_MODEL, D_FF), BF16),
            jax.ShapeDtypeStruct((D_FF, D_MODEL), BF16),
        ],
        scratch_shapes=[
            pltpu.VMEM((ts + HALO, CONV_WIDTH), F32),
            pltpu.VMEM((ts, d), BF16),
            pltpu.VMEM((RET_HEADS, HEAD_DIM, HEAD_DIM), F32),
        ],
        compiler_params=pltpu.CompilerParams(
            dimension_semantics=("arbitrary", "arbitrary"),
            vmem_limit_bytes=VMEM_LIMIT_BYTES),
        name=f"mixer_{layer}",
    )(x, mod, norm_g, w_in, conv_w, ret_g, w_out, cos2, sin2, dec, xi, zeta, cd,
      w_up, w_down)


def _mlp_kernel(x_ref, mod_ref, g_ref, w_up_ref, w_dn_ref, fg_ref, *rest,
                layer, final, cast_next):
    o_ref = rest[2] if cast_next else rest[0]
    x = x_ref[...]
    mod = mod_ref[pl.ds(pl.program_id(0), 1), :]
    shift = mod[:, 3 * D_MODEL:4 * D_MODEL]
    scale = mod[:, 4 * D_MODEL:5 * D_MODEL]
    gate = mod[:, 5 * D_MODEL:6 * D_MODEL]
    gain = g_ref[pl.ds(layer, 1), :]
    ts = x.shape[0]
    row_chunks = [slice(i * (ts // HEAD_ROW_CHUNKS), (i + 1) * (ts // HEAD_ROW_CHUNKS))
                  for i in range(HEAD_ROW_CHUNKS)]
    hb_rows = [_modulated_rms(x[rows], gain, scale, shift).astype(BF16)
               for rows in row_chunks]
    hb = jnp.concatenate(hb_rows, axis=0)
    acc = None
    for ci in range(D_FF // FF_CHUNK):
        cols = slice(ci * FF_CHUNK, (ci + 1) * FF_CHUNK)
        if ci == 0:
            up = jnp.concatenate(
                [jnp.dot(h, w_up_ref[:, cols], preferred_element_type=F32)
                 for h in hb_rows], axis=0)
        else:
            up = jnp.dot(hb, w_up_ref[:, cols], preferred_element_type=F32)
        act = jnp.square(jnp.maximum(up, 0.0)).astype(BF16)
        if final and ci == D_FF // FF_CHUNK - 1:
            down = jnp.concatenate(
                [jnp.dot(act[rows], w_dn_ref[cols, :], preferred_element_type=F32)
                 for rows in row_chunks], axis=0)
        else:
            down = jnp.dot(act, w_dn_ref[cols, :], preferred_element_type=F32)
        acc = down if acc is None else acc + down
    y = x + gate * acc
    if final:
        r = lax.rsqrt(jnp.mean(y * y, axis=-1, keepdims=True) + EPS)
        y = (y * r) * fg_ref[...]
    o_ref[...] = y

    if cast_next:
        w_in_f32_ref, w_out_f32_ref, _, w_in_bf16_ref, w_out_bf16_ref = rest
        w_in_bf16_ref[...] = w_in_f32_ref[...].astype(BF16)
        w_out_bf16_ref[...] = w_out_f32_ref[...].astype(BF16)


def _mlp(layer, x, mod, norm_g, w_up, w_down, final_g, w_in, w_out):
    nb, t, d = x.shape
    ts = MLP_TILE
    ns = t // ts
    final = layer == DEPTH - 1
    const2 = lambda b, s: (0, 0)
    in_specs = [
        pl.BlockSpec((None, ts, d), lambda b, s: (b, s, 0)),
        _resident((None, nb, N_MOD * d), lambda b, s: (layer, 0, 0)),
        _resident((DEPTH, d), const2),
        _resident((d, D_FF), const2),
        _resident((D_FF, d), const2),
        _resident((1, d), const2),
    ]
    operands = [x, mod, norm_g, w_up, w_down, final_g]
    out_specs = [pl.BlockSpec((None, ts, d), lambda b, s: (b, s, 0))]
    out_shape = [jax.ShapeDtypeStruct(x.shape, F32)]
    if not final:
        rows = _cast_rows(D_MODEL, nb * ns)
        step_rows = lambda b, s: (b * ns + s, 0)
        next_rows = lambda b, s: (layer + 1, b * ns + s, 0)
        in_specs += [pl.BlockSpec((None, rows, IN_COLS), next_rows),
                     pl.BlockSpec((None, rows, d), next_rows)]
        operands += [w_in, w_out]
        out_specs += [pl.BlockSpec((rows, IN_COLS), step_rows),
                      pl.BlockSpec((rows, d), step_rows)]
        out_shape += [jax.ShapeDtypeStruct((D_MODEL, IN_COLS), BF16),
                      jax.ShapeDtypeStruct((D_MODEL, D_MODEL), BF16)]
    return pl.pallas_call(
        functools.partial(_mlp_kernel, layer=layer, final=final, cast_next=not final),
        grid=(nb, ns),
        in_specs=in_specs,
        out_specs=out_specs,
        out_shape=out_shape,
        compiler_params=pltpu.CompilerParams(
            dimension_semantics=("arbitrary", "arbitrary"),
            vmem_limit_bytes=VMEM_LIMIT_BYTES),
        name=f"mlp_{layer}",
    )(*operands)


def _tables(t):
    half = HEAD_DIM // 2
    inv_freq = ROPE_BASE ** (-np.arange(0, HEAD_DIM, 2, dtype=np.float64) / HEAD_DIM)
    ang = np.arange(t, dtype=np.float64)[:, None] * inv_freq[None, :]
    cos, sin = np.cos(ang), np.sin(ang)
    assert cos.shape == (t, half)
    cos2 = np.concatenate([cos, cos], axis=-1)
    sin2 = np.concatenate([-sin, sin], axis=-1)

    rc = RET_CHUNK
    k_scale = HEAD_DIM ** -0.5
    log_g = np.log1p(-np.exp2(-5.0 - np.arange(RET_HEADS, dtype=np.float64)))
    idx = np.arange(rc, dtype=np.float64)
    diff = idx[:, None] - idx[None, :]
    causal = diff >= 0
    dec = np.where(causal[None],
                   np.exp(log_g[:, None, None] * np.where(causal, diff, 0.0)[None]),
                   0.0) * k_scale
    xi = np.exp(log_g[:, None] * (idx + 1.0)[None])
    zeta = np.exp(log_g[:, None] * (rc - 1.0 - idx)[None]) * k_scale
    cd = np.exp(log_g * rc)
    lanes = (RET_HEADS, rc, HEAD_DIM)
    tables = (cos2, sin2, np.swapaxes(dec, 1, 2),
              np.broadcast_to(xi[:, :, None], lanes),
              np.broadcast_to(zeta[:, None, :], (RET_HEADS, F32_SUBLANES, rc)),
              np.broadcast_to(cd[:, None, None], (RET_HEADS, F32_SUBLANES, HEAD_DIM)))
    return tuple(jnp.asarray(a, dtype=F32) for a in tables)


def kernel(x, c, norm1_g, w_in, conv_w, ret_norm_g, w_out, norm2_g, w_up, w_down,
           w_ada, b_ada, final_g):
    nb, t, d = x.shape
    assert d == D_MODEL and t % MIX_TILE == 0 and t % MLP_TILE == 0
    assert MIX_TILE % RET_CHUNK == 0
    assert w_in.shape == (DEPTH, D_MODEL, IN_COLS)
    mod, w_in_b, w_out_b = _adaln(c, w_ada, b_ada, w_in, w_out)
    tables = _tables(t)
    fg = final_g.reshape(1, D_MODEL)
    for layer in range(DEPTH):
        x, w_up_b, w_down_b = _mixer(layer, x, mod, norm1_g, w_in_b, conv_w, ret_norm_g,
                                     w_out_b, w_up, w_down, tables)
        outs = _mlp(layer, x, mod, norm2_g, w_up_b, w_down_b, fg, w_in, w_out)
        if layer < DEPTH - 1:
            x, w_in_b, w_out_b = outs
        else:
            x, = outs
    return x
```

```python
import functools

import jax
import jax.numpy as jnp
import numpy as np
from jax import lax
from jax.experimental import pallas as pl
from jax.experimental.pallas import tpu as pltpu

D_MODEL = 1024
DEPTH = 4
CONV_WIDTH = 512
CONV_K = 3
RET_HEADS = 4
RET_WIDTH = 512
HEAD_DIM = 128
D_FF = 4 * D_MODEL
ROPE_BASE = 10000.0
EPS = 1e-6
N_MOD = 6
SEG = 512
IN_COLS = 7 * SEG

MIX_TILE = 1024
MLP_TILE = 1024
RET_CHUNK = 256
FF_CHUNK = 1024
ADA_COLS = 1536
LOOKAHEAD = 2
HEAD_ROW_CHUNKS = 4
MIX_HEAD_ROW_CHUNKS = 2
VMEM_LIMIT_BYTES = 48 * 1024 * 1024

F32 = jnp.float32
BF16 = jnp.bfloat16
F32_SUBLANES = 8
BF16_SUBLANES = 16
HALO = F32_SUBLANES


def _resident(block_shape, index_map):
    return pl.BlockSpec(block_shape, index_map, pipeline_mode=pl.Buffered(1))


def _adaln_kernel(c_ref, w_ref, b_ref, w_in_f32_ref, w_out_f32_ref,
                  o_ref, w_in_bf16_ref, w_out_bf16_ref):
    w_in_bf16_ref[...] = w_in_f32_ref[...].astype(BF16)
    w_out_bf16_ref[...] = w_out_f32_ref[...].astype(BF16)

    c = c_ref[...]
    ca = c * jax.nn.sigmoid(c)
    ca_hi = ca.astype(BF16).astype(F32)
    lhs = jnp.concatenate([ca_hi, ca - ca_hi], axis=0).astype(BF16)
    w = w_ref[...]
    w_hi = w.astype(BF16)
    w_lo = (w - w_hi.astype(F32)).astype(BF16)
    acc = (jnp.dot(lhs, w_hi, preferred_element_type=F32)
           + jnp.dot(lhs, w_lo, preferred_element_type=F32))
    nb = c.shape[0]
    bias = b_ref[pl.ds(pl.program_id(0), 1), :]
    o_ref[...] = acc[0:nb] + acc[nb:2 * nb] + bias


def _adaln(c, w_ada, b_ada, w_in, w_out):
    nb = c.shape[0]
    n_col = N_MOD * D_MODEL // ADA_COLS
    rows = _cast_rows(D_MODEL, DEPTH * n_col)
    step_rows = lambda l, j: (l * n_col + j, 0)
    first_rows = lambda l, j: (0, l * n_col + j, 0)
    return pl.pallas_call(
        _adaln_kernel,
        grid=(DEPTH, n_col),
        in_specs=[
            pl.BlockSpec((nb, D_MODEL), lambda l, j: (0, 0)),
            pl.BlockSpec((None, D_MODEL, ADA_COLS), lambda l, j: (l, 0, j)),
            pl.BlockSpec((DEPTH, ADA_COLS), lambda l, j: (0, j)),
            pl.BlockSpec((None, rows, IN_COLS), first_rows),
            pl.BlockSpec((None, rows, D_MODEL), first_rows),
        ],
        out_specs=[
            pl.BlockSpec((None, nb, ADA_COLS), lambda l, j: (l, 0, j)),
            pl.BlockSpec((rows, IN_COLS), step_rows),
            pl.BlockSpec((rows, D_MODEL), step_rows),
        ],
        out_shape=[
            jax.ShapeDtypeStruct((DEPTH, nb, N_MOD * D_MODEL), F32),
            jax.ShapeDtypeStruct((D_MODEL, IN_COLS), BF16),
            jax.ShapeDtypeStruct((D_MODEL, D_MODEL), BF16),
        ],
        compiler_params=pltpu.CompilerParams(
            dimension_semantics=("arbitrary", "arbitrary"),
            vmem_limit_bytes=VMEM_LIMIT_BYTES),
        name="adaln_mod",
    )(c, w_ada, b_ada, w_in, w_out)


def _modulated_rms(x, gain, scale, shift):
    r = lax.rsqrt(jnp.mean(x * x, axis=-1, keepdims=True) + EPS)
    return ((x * r) * gain) * (1.0 + scale) + shift


def _mixer_kernel(x_ref, mod_ref, g_ref, w_in_ref, cw_ref, rg_ref, w_out_ref,
                  cos_ref, sin_ref, dec_ref, xi_ref, zeta_ref, cd_ref,
                  w_up_f32_ref, w_dn_f32_ref,
                  o_ref, w_up_bf16_ref, w_dn_bf16_ref, ubuf, ybuf, state, *, layer):
    ts = x_ref.shape[0]

    @pl.when(pl.program_id(1) == 0)
    def _start_of_sequence():
        ubuf[0:HALO, :] = jnp.zeros((HALO, CONV_WIDTH), F32)
        state[...] = jnp.zeros(state.shape, F32)

    x = x_ref[...]
    mod = mod_ref[pl.ds(pl.program_id(0), 1), :]
    shift = mod[:, 0:D_MODEL]
    scale = mod[:, D_MODEL:2 * D_MODEL]
    gate = mod[:, 2 * D_MODEL:3 * D_MODEL]
    gain = g_ref[pl.ds(layer, 1), :]
    n_head = MIX_HEAD_ROW_CHUNKS
    hb_rows = [_modulated_rms(x[i * (ts // n_head):(i + 1) * (ts // n_head)],
                              gain, scale, shift).astype(BF16) for i in range(n_head)]
    hb = jnp.concatenate(hb_rows, axis=0)

    def proj(i):
        return jnp.dot(hb, w_in_ref[:, i * SEG:(i + 1) * SEG],
                       preferred_element_type=F32)

    q = jnp.concatenate(
        [jnp.dot(h, w_in_ref[:, 3 * SEG:4 * SEG], preferred_element_type=F32)
         for h in hb_rows], axis=0)
    k = proj(4)
    v = proj(5)

    u = proj(1) * proj(2)
    ubuf[HALO:HALO + ts, :] = u
    u1 = ubuf[HALO - 1:HALO - 1 + ts, :]
    u2 = ubuf[HALO - 2:HALO - 2 + ts, :]
    cw = cw_ref[...]
    conv = u2 * cw[0:1, :] + u1 * cw[1:2, :] + u * cw[2:3, :]
    ubuf[0:HALO, :] = ubuf[ts:ts + HALO, :]

    cos2 = cos_ref[...]
    sin2 = sin_ref[...]
    rc = dec_ref.shape[1]
    n_chunks = ts // rc
    pairs = [(hh, ci) for hh in range(RET_HEADS) for ci in range(n_chunks)]
    head_cols = [slice(hh * HEAD_DIM, (hh + 1) * HEAD_DIM) for hh in range(RET_HEADS)]
    chunk_rows = [slice(ci * rc, (ci + 1) * rc) for ci in range(n_chunks)]
    qh, kh = [], []
    for cols in head_cols:
        qc, kc = q[:, cols], k[:, cols]
        qh.append(qc * cos2 + pltpu.roll(qc, HEAD_DIM // 2, 1) * sin2)
        kh.append(kc * cos2 + pltpu.roll(kc, HEAD_DIM // 2, 1) * sin2)

    gr = proj(6)

    scores_t, kv_t, v_t = {}, {}, {}
    s_now = [state[hh] for hh in range(RET_HEADS)]
    for step in range(len(pairs) + LOOKAHEAD):
        if step < len(pairs):
            hh, ci = pairs[step]
            rows = chunk_rows[ci]
            kc = kh[hh][rows].astype(BF16)
            scores_t[hh, ci] = lax.dot_general(
                kc, qh[hh][rows].astype(BF16),
                (((1,), (1,)), ((), ())), preferred_element_type=F32)
            vt = v[rows, head_cols[hh]].T
            v_t[hh, ci] = vt.astype(BF16)
            kv_t[hh, ci] = jnp.dot((vt * zeta_ref[hh][0:1, :]).astype(BF16), kc,
                                   preferred_element_type=F32)
        if step >= LOOKAHEAD:
            hh, ci = pairs[step - LOOKAHEAD]
            rows, cols = chunk_rows[ci], head_cols[hh]
            p_t = (scores_t.pop((hh, ci)) * dec_ref[hh]).astype(BF16)
            inner_t = jnp.dot(v_t.pop((hh, ci)), p_t, preferred_element_type=F32)
            cross_t = lax.dot_general(
                s_now[hh].astype(BF16), (qh[hh][rows] * xi_ref[hh]).astype(BF16),
                (((1,), (1,)), ((), ())), preferred_element_type=F32)
            s_now[hh] = s_now[hh] * cd_ref[hh][0:1, :] + kv_t.pop((hh, ci))
            o = (inner_t + cross_t).T
            mu = jnp.mean(o, axis=-1, keepdims=True)
            oc = o - mu
            var = jnp.mean(oc * oc, axis=-1, keepdims=True)
            on = (oc * lax.rsqrt(var + EPS)) * rg_ref[pl.ds(layer, 1), cols]
            g = gr[rows, cols]
            y = (g * jax.nn.sigmoid(g)) * on
            ybuf[rows, CONV_WIDTH + hh * HEAD_DIM:
                 CONV_WIDTH + (hh + 1) * HEAD_DIM] = y.astype(BF16)
    for hh in range(RET_HEADS):
        state[hh] = s_now[hh]

    cb = proj(0)
    ybuf[:, 0:CONV_WIDTH] = (cb * conv).astype(BF16)

    mix = (jnp.dot(ybuf[:, 0:CONV_WIDTH], w_out_ref[0:CONV_WIDTH, :],
                   preferred_element_type=F32)
           + jnp.dot(ybuf[:, CONV_WIDTH:], w_out_ref[CONV_WIDTH:, :],
                     preferred_element_type=F32))
    o_ref[...] = x + gate * mix

    w_up_bf16_ref[...] = w_up_f32_ref[...].astype(BF16)
    w_dn_bf16_ref[...] = w_dn_f32_ref[...].astype(BF16)


def _cast_rows(n_rows, n_steps):
    rows = n_rows // n_steps
    assert rows * n_steps == n_rows and rows % BF16_SUBLANES == 0
    return rows


def _mixer(layer, x, mod, norm_g, w_in, conv_w, ret_g, w_out, w_up, w_down, tables):
    nb, t, d = x.shape
    cos2, sin2, dec, xi, zeta, cd = tables
    rc = dec.shape[1]
    ts = MIX_TILE
    ns = t // ts
    up_rows, dn_rows = _cast_rows(D_MODEL, nb * ns), _cast_rows(D_FF, nb * ns)
    const2 = lambda b, s: (0, 0)
    const3 = lambda b, s: (0, 0, 0)
    lay2 = lambda b, s: (layer, 0, 0)
    return pl.pallas_call(
        functools.partial(_mixer_kernel, layer=layer),
        grid=(nb, ns),
        in_specs=[
            pl.BlockSpec((None, ts, d), lambda b, s: (b, s, 0)),
            _resident((None, nb, N_MOD * d), lay2),
            _resident((DEPTH, d), const2),
            _resident((d, IN_COLS), const2),
            _resident((None, CONV_K, CONV_WIDTH), lay2),
            _resident((DEPTH, RET_WIDTH), const2),
            _resident((d, d), const2),
            pl.BlockSpec((ts, HEAD_DIM), lambda b, s: (s, 0)),
            pl.BlockSpec((ts, HEAD_DIM), lambda b, s: (s, 0)),
            _resident((RET_HEADS, rc, rc), const3),
            _resident((RET_HEADS, rc, HEAD_DIM), const3),
            _resident((RET_HEADS, F32_SUBLANES, rc), const3),
            _resident((RET_HEADS, F32_SUBLANES, HEAD_DIM), const3),
            pl.BlockSpec((None, up_rows, D_FF), lambda b, s: (layer, b * ns + s, 0)),
            pl.BlockSpec((None, dn_rows, d), lambda b, s: (layer, b * ns + s, 0)),
        ],
        out_specs=[
            pl.BlockSpec((None, ts, d), lambda b, s: (b, s, 0)),
            pl.BlockSpec((up_rows, D_FF), lambda b, s: (b * ns + s, 0)),
            pl.BlockSpec((dn_rows, d), lambda b, s: (b * ns + s, 0)),
        ],
        out_shape=[
            jax.ShapeDtypeStruct(x.shape, F32),
            jax.ShapeDtypeStruct((D_MODEL, D_FF), BF16),
            jax.ShapeDtypeStruct((D_FF, D_MODEL), BF16),
        ],
        scratch_shapes=[
            pltpu.VMEM((ts + HALO, CONV_WIDTH), F32),
            pltpu.VMEM((ts, d), BF16),
            pltpu.VMEM((RET_HEADS, HEAD_DIM, HEAD_DIM), F32),
        ],
        compiler_params=pltpu.CompilerParams(
            dimension_semantics=("arbitrary", "arbitrary"),
            vmem_limit_bytes=VMEM_LIMIT_BYTES),
        name=f"mixer_{layer}",
    )(x, mod, norm_g, w_in, conv_w, ret_g, w_out, cos2, sin2, dec, xi, zeta, cd,
      w_up, w_down)


def _mlp_kernel(x_ref, mod_ref, g_ref, w_up_ref, w_dn_ref, fg_ref, *rest,
                layer, final, cast_next):
    o_ref = rest[2] if cast_next else rest[0]
    x = x_ref[...]
    mod = mod_ref[pl.ds(pl.program_id(0), 1), :]
    shift = mod[:, 3 * D_MODEL:4 * D_MODEL]
    scale = mod[:, 4 * D_MODEL:5 * D_MODEL]
    gate = mod[:, 5 * D_MODEL:6 * D_MODEL]
    gain = g_ref[pl.ds(layer, 1), :]
    ts = x.shape[0]
    row_chunks = [slice(i * (ts // HEAD_ROW_CHUNKS), (i + 1) * (ts // HEAD_ROW_CHUNKS))
                  for i in range(HEAD_ROW_CHUNKS)]
    hb_rows = [_modulated_rms(x[rows], gain, scale, shift).astype(BF16)
               for rows in row_chunks]
    hb = jnp.concatenate(hb_rows, axis=0)
    acc = None
    for ci in range(D_FF // FF_CHUNK):
        cols = slice(ci * FF_CHUNK, (ci + 1) * FF_CHUNK)
        if ci == 0:
            up = jnp.concatenate(
                [jnp.dot(h, w_up_ref[:, cols], preferred_element_type=F32)
                 for h in hb_rows], axis=0)
        else:
            up = jnp.dot(hb, w_up_ref[:, cols], preferred_element_type=F32)
        act = jnp.square(jnp.maximum(up, 0.0)).astype(BF16)
        if final and ci == D_FF // FF_CHUNK - 1:
            down = jnp.concatenate(
                [jnp.dot(act[rows], w_dn_ref[cols, :], preferred_element_type=F32)
                 for rows in row_chunks], axis=0)
        else:
            down = jnp.dot(act, w_dn_ref[cols, :], preferred_element_type=F32)
        acc = down if acc is None else acc + down
    y = x + gate * acc
    if final:
        r = lax.rsqrt(jnp.mean(y * y, axis=-1, keepdims=True) + EPS)
        y = (y * r) * fg_ref[...]
    o_ref[...] = y

    if cast_next:
        w_in_f32_ref, w_out_f32_ref, _, w_in_bf16_ref, w_out_bf16_ref = rest
        w_in_bf16_ref[...] = w_in_f32_ref[...].astype(BF16)
        w_out_bf16_ref[...] = w_out_f32_ref[...].astype(BF16)


def _mlp(layer, x, mod, norm_g, w_up, w_down, final_g, w_in, w_out):
    nb, t, d = x.shape
    ts = MLP_TILE
    ns = t // ts
    final = layer == DEPTH - 1
    const2 = lambda b, s: (0, 0)
    in_specs = [
        pl.BlockSpec((None, ts, d), lambda b, s: (b, s, 0)),
        _resident((None, nb, N_MOD * d), lambda b, s: (layer, 0, 0)),
        _resident((DEPTH, d), const2),
        _resident((d, D_FF), const2),
        _resident((D_FF, d), const2),
        _resident((1, d), const2),
    ]
    operands = [x, mod, norm_g, w_up, w_down, final_g]
    out_specs = [pl.BlockSpec((None, ts, d), lambda b, s: (b, s, 0))]
    out_shape = [jax.ShapeDtypeStruct(x.shape, F32)]
    if not final:
        rows = _cast_rows(D_MODEL, nb * ns)
        step_rows = lambda b, s: (b * ns + s, 0)
        next_rows = lambda b, s: (layer + 1, b * ns + s, 0)
        in_specs += [pl.BlockSpec((None, rows, IN_COLS), next_rows),
                     pl.BlockSpec((None, rows, d), next_rows)]
        operands += [w_in, w_out]
        out_specs += [pl.BlockSpec((rows, IN_COLS), step_rows),
                      pl.BlockSpec((rows, d), step_rows)]
        out_shape += [jax.ShapeDtypeStruct((D_MODEL, IN_COLS), BF16),
                      jax.ShapeDtypeStruct((D_MODEL, D_MODEL), BF16)]
    return pl.pallas_call(
        functools.partial(_mlp_kernel, layer=layer, final=final, cast_next=not final),
        grid=(nb, ns),
        in_specs=in_specs,
        out_specs=out_specs,
        out_shape=out_shape,
        compiler_params=pltpu.CompilerParams(
            dimension_semantics=("arbitrary", "arbitrary"),
            vmem_limit_bytes=VMEM_LIMIT_BYTES),
        name=f"mlp_{layer}",
    )(*operands)


def _tables(t):
    half = HEAD_DIM // 2
    inv_freq = ROPE_BASE ** (-np.arange(0, HEAD_DIM, 2, dtype=np.float64) / HEAD_DIM)
    ang = np.arange(t, dtype=np.float64)[:, None] * inv_freq[None, :]
    cos, sin = np.cos(ang), np.sin(ang)
    assert cos.shape == (t, half)
    cos2 = np.concatenate([cos, cos], axis=-1)
    sin2 = np.concatenate([-sin, sin], axis=-1)

    rc = RET_CHUNK
    k_scale = HEAD_DIM ** -0.5
    log_g = np.log1p(-np.exp2(-5.0 - np.arange(RET_HEADS, dtype=np.float64)))
    idx = np.arange(rc, dtype=np.float64)
    diff = idx[:, None] - idx[None, :]
    causal = diff >= 0
    dec = np.where(causal[None],
                   np.exp(log_g[:, None, None] * np.where(causal, diff, 0.0)[None]),
                   0.0) * k_scale
    xi = np.exp(log_g[:, None] * (idx + 1.0)[None])
    zeta = np.exp(log_g[:, None] * (rc - 1.0 - idx)[None]) * k_scale
    cd = np.exp(log_g * rc)
    lanes = (RET_HEADS, rc, HEAD_DIM)
    tables = (cos2, sin2, np.swapaxes(dec, 1, 2),
              np.broadcast_to(xi[:, :, None], lanes),
              np.broadcast_to(zeta[:, None, :], (RET_HEADS, F32_SUBLANES, rc)),
              np.broadcast_to(cd[:, None, None], (RET_HEADS, F32_SUBLANES, HEAD_DIM)))
    return tuple(jnp.asarray(a, dtype=F32) for a in tables)


def kernel(x, c, norm1_g, w_in, conv_w, ret_norm_g, w_out, norm2_g, w_up, w_down,
           w_ada, b_ada, final_g):
    nb, t, d = x.shape
    assert d == D_MODEL and t % MIX_TILE == 0 and t % MLP_TILE == 0
    assert MIX_TILE % RET_CHUNK == 0
    assert w_in.shape == (DEPTH, D_MODEL, IN_COLS)
    mod, w_in_b, w_out_b = _adaln(c, w_ada, b_ada, w_in, w_out)
    tables = _tables(t)
    fg = final_g.reshape(1, D_MODEL)
    for layer in range(DEPTH):
        x, w_up_b, w_down_b = _mixer(layer, x, mod, norm1_g, w_in_b, conv_w, ret_norm_g,
                                     w_out_b, w_up, w_down, tables)
        outs = _mlp(layer, x, mod, norm2_g, w_up_b, w_down_b, fg, w_in, w_out)
        if layer < DEPTH - 1:
            x, w_in_b, w_out_b = outs
        else:
            x, = outs
    return x
```

```python
import functools

import jax
import jax.numpy as jnp
import numpy as np
from jax import lax
from jax.experimental import pallas as pl
from jax.experimental.pallas import tpu as pltpu

D_MODEL = 1024
DEPTH = 4
CONV_WIDTH = 512
CONV_K = 3
RET_HEADS = 4
RET_WIDTH = 512
HEAD_DIM = 128
D_FF = 4 * D_MODEL
ROPE_BASE = 10000.0
EPS = 1e-6
N_MOD = 6
SEG = 512
IN_COLS = 7 * SEG

MIX_TILE = 1024
MLP_TILE = 1024
RET_CHUNK = 256
FF_CHUNK = 1024
ADA_COLS = 1536
LOOKAHEAD = 2
HEAD_ROW_CHUNKS = 4
MIX_HEAD_ROW_CHUNKS = 2
VMEM_LIMIT_BYTES = 48 * 1024 * 1024
MIXER_VMEM_LIMIT_BYTES = 56 * 1024 * 1024

F32 = jnp.float32
BF16 = jnp.bfloat16
F32_SUBLANES = 8
BF16_SUBLANES = 16
HALO = F32_SUBLANES


def _resident(block_shape, index_map):
    return pl.BlockSpec(block_shape, index_map, pipeline_mode=pl.Buffered(1))


def _adaln_kernel(c_ref, w_ref, b_ref, w_in_f32_ref, w_out_f32_ref,
                  o_ref, w_in_bf16_ref, w_out_bf16_ref):
    w_in_bf16_ref[...] = w_in_f32_ref[...].astype(BF16)
    w_out_bf16_ref[...] = w_out_f32_ref[...].astype(BF16)

    c = c_ref[...]
    ca = c * jax.nn.sigmoid(c)
    ca_hi = ca.astype(BF16).astype(F32)
    lhs = jnp.concatenate([ca_hi, ca - ca_hi], axis=0).astype(BF16)
    w = w_ref[...]
    w_hi = w.astype(BF16)
    w_lo = (w - w_hi.astype(F32)).astype(BF16)
    acc = (jnp.dot(lhs, w_hi, preferred_element_type=F32)
           + jnp.dot(lhs, w_lo, preferred_element_type=F32))
    nb = c.shape[0]
    bias = b_ref[pl.ds(pl.program_id(0), 1), :]
    o_ref[...] = acc[0:nb] + acc[nb:2 * nb] + bias


def _adaln(c, w_ada, b_ada, w_in, w_out):
    nb = c.shape[0]
    n_col = N_MOD * D_MODEL // ADA_COLS
    rows = _cast_rows(D_MODEL, DEPTH * n_col)
    step_rows = lambda l, j: (l * n_col + j, 0)
    first_rows = lambda l, j: (0, l * n_col + j, 0)
    return pl.pallas_call(
        _adaln_kernel,
        grid=(DEPTH, n_col),
        in_specs=[
            pl.BlockSpec((nb, D_MODEL), lambda l, j: (0, 0)),
            pl.BlockSpec((None, D_MODEL, ADA_COLS), lambda l, j: (l, 0, j)),
            pl.BlockSpec((DEPTH, ADA_COLS), lambda l, j: (0, j)),
            pl.BlockSpec((None, rows, IN_COLS), first_rows),
            pl.BlockSpec((None, rows, D_MODEL), first_rows),
        ],
        out_specs=[
            pl.BlockSpec((None, nb, ADA_COLS), lambda l, j: (l, 0, j)),
            pl.BlockSpec((rows, IN_COLS), step_rows),
            pl.BlockSpec((rows, D_MODEL), step_rows),
        ],
        out_shape=[
            jax.ShapeDtypeStruct((DEPTH, nb, N_MOD * D_MODEL), F32),
            jax.ShapeDtypeStruct((D_MODEL, IN_COLS), BF16),
            jax.ShapeDtypeStruct((D_MODEL, D_MODEL), BF16),
        ],
        compiler_params=pltpu.CompilerParams(
            dimension_semantics=("arbitrary", "arbitrary"),
            vmem_limit_bytes=VMEM_LIMIT_BYTES),
        name="adaln_mod",
    )(c, w_ada, b_ada, w_in, w_out)


def _modulated_rms(x, gain, scale, shift):
    r = lax.rsqrt(jnp.mean(x * x, axis=-1, keepdims=True) + EPS)
    return ((x * r) * gain) * (1.0 + scale) + shift


def _mixer_kernel(x_ref, mod_ref, g_ref, w_in_ref, cw_ref, rg_ref, w_out_ref,
                  cos_ref, sin_ref, dec_ref, xi_ref, zeta_ref, cd_ref,
                  w_up_f32_ref, w_dn_f32_ref,
                  o_ref, w_up_bf16_ref, w_dn_bf16_ref, ubuf, ybuf, state, *, layer):
    ts = x_ref.shape[0]

    @pl.when(pl.program_id(1) == 0)
    def _start_of_sequence():
        ubuf[0:HALO, :] = jnp.zeros((HALO, CONV_WIDTH), F32)
        state[...] = jnp.zeros(state.shape, F32)

    x = x_ref[...]
    mod = mod_ref[pl.ds(pl.program_id(0), 1), :]
    shift = mod[:, 0:D_MODEL]
    scale = mod[:, D_MODEL:2 * D_MODEL]
    gate = mod[:, 2 * D_MODEL:3 * D_MODEL]
    gain = g_ref[pl.ds(layer, 1), :]
    n_head = MIX_HEAD_ROW_CHUNKS
    hb_rows = [_modulated_rms(x[i * (ts // n_head):(i + 1) * (ts // n_head)],
                              gain, scale, shift).astype(BF16) for i in range(n_head)]
    hb = jnp.concatenate(hb_rows, axis=0)

    def proj(i):
        return jnp.dot(hb, w_in_ref[:, i * SEG:(i + 1) * SEG],
                       preferred_element_type=F32)

    q = jnp.concatenate(
        [jnp.dot(h, w_in_ref[:, 3 * SEG:4 * SEG], preferred_element_type=F32)
         for h in hb_rows], axis=0)
    k = proj(4)
    v = proj(5)

    u = proj(1) * proj(2)
    ubuf[HALO:HALO + ts, :] = u
    u1 = ubuf[HALO - 1:HALO - 1 + ts, :]
    u2 = ubuf[HALO - 2:HALO - 2 + ts, :]
    cw = cw_ref[...]
    conv = u2 * cw[0:1, :] + u1 * cw[1:2, :] + u * cw[2:3, :]
    ubuf[0:HALO, :] = ubuf[ts:ts + HALO, :]

    cos2 = cos_ref[...]
    sin2 = sin_ref[...]
    rc = dec_ref.shape[1]
    n_chunks = ts // rc
    pairs = [(hh, ci) for hh in range(RET_HEADS) for ci in range(n_chunks)]
    head_cols = [slice(hh * HEAD_DIM, (hh + 1) * HEAD_DIM) for hh in range(RET_HEADS)]
    chunk_rows = [slice(ci * rc, (ci + 1) * rc) for ci in range(n_chunks)]
    qh, kh = [], []
    for cols in head_cols:
        qc, kc = q[:, cols], k[:, cols]
        qh.append(qc * cos2 + pltpu.roll(qc, HEAD_DIM // 2, 1) * sin2)
        kh.append(kc * cos2 + pltpu.roll(kc, HEAD_DIM // 2, 1) * sin2)

    gr = proj(6)

    scores_t, kv_t, v_t = {}, {}, {}
    s_now = [state[hh] for hh in range(RET_HEADS)]
    for step in range(len(pairs) + LOOKAHEAD):
        if step < len(pairs):
            hh, ci = pairs[step]
            rows = chunk_rows[ci]
            kc = kh[hh][rows].astype(BF16)
            scores_t[hh, ci] = lax.dot_general(
                kc, qh[hh][rows].astype(BF16),
                (((1,), (1,)), ((), ())), preferred_element_type=F32)
            vt = v[rows, head_cols[hh]].T
            v_t[hh, ci] = vt.astype(BF16)
            kv_t[hh, ci] = jnp.dot((vt * zeta_ref[hh][0:1, :]).astype(BF16), kc,
                                   preferred_element_type=F32)
        if step >= LOOKAHEAD:
            hh, ci = pairs[step - LOOKAHEAD]
            rows, cols = chunk_rows[ci], head_cols[hh]
            p_t = (scores_t.pop((hh, ci)) * dec_ref[hh]).astype(BF16)
            inner_t = jnp.dot(v_t.pop((hh, ci)), p_t, preferred_element_type=F32)
            cross_t = lax.dot_general(
                s_now[hh].astype(BF16), (qh[hh][rows] * xi_ref[hh]).astype(BF16),
                (((1,), (1,)), ((), ())), preferred_element_type=F32)
            s_now[hh] = s_now[hh] * cd_ref[hh][0:1, :] + kv_t.pop((hh, ci))
            o = (inner_t + cross_t).T
            mu = jnp.mean(o, axis=-1, keepdims=True)
            oc = o - mu
            var = jnp.mean(oc * oc, axis=-1, keepdims=True)
            on = (oc * lax.rsqrt(var + EPS)) * rg_ref[pl.ds(layer, 1), cols]
            g = gr[rows, cols]
            y = (g * jax.nn.sigmoid(g)) * on
            ybuf[rows, CONV_WIDTH + hh * HEAD_DIM:
                 CONV_WIDTH + (hh + 1) * HEAD_DIM] = y.astype(BF16)
    for hh in range(RET_HEADS):
        state[hh] = s_now[hh]

    cb = proj(0)
    ybuf[:, 0:CONV_WIDTH] = (cb * conv).astype(BF16)

    mix = (jnp.dot(ybuf[:, 0:CONV_WIDTH], w_out_ref[0:CONV_WIDTH, :],
                   preferred_element_type=F32)
           + jnp.dot(ybuf[:, CONV_WIDTH:], w_out_ref[CONV_WIDTH:, :],
                     preferred_element_type=F32))
    o_ref[...] = x + gate * mix

    w_up_bf16_ref[...] = w_up_f32_ref[...].astype(BF16)
    w_dn_bf16_ref[...] = w_dn_f32_ref[...].astype(BF16)


def _cast_rows(n_rows, n_steps):
    rows = n_rows // n_steps
    assert rows * n_steps == n_rows and rows % BF16_SUBLANES == 0
    return rows


def _mixer(layer, x, mod, norm_g, w_in, conv_w, ret_g, w_out, w_up, w_down, tables):
    nb, t, d = x.shape
    cos2, sin2, dec, xi, zeta, cd = tables
    rc = dec.shape[1]
    ts = MIX_TILE
    ns = t // ts
    up_rows, dn_rows = _cast_rows(D_MODEL, nb * ns), _cast_rows(D_FF, nb * ns)
    const2 = lambda b, s: (0, 0)
    const3 = lambda b, s: (0, 0, 0)
    lay2 = lambda b, s: (layer, 0, 0)
    return pl.pallas_call(
        functools.partial(_mixer_kernel, layer=layer),
        grid=(nb, ns),
        in_specs=[
            pl.BlockSpec((None, ts, d), lambda b, s: (b, s, 0)),
            _resident((None, nb, N_MOD * d), lay2),
            _resident((DEPTH, d), const2),
            _resident((d, IN_COLS), const2),
            _resident((None, CONV_K, CONV_WIDTH), lay2),
            _resident((DEPTH, RET_WIDTH), const2),
            _resident((d, d), const2),
            pl.BlockSpec((ts, HEAD_DIM), lambda b, s: (s, 0)),
            pl.BlockSpec((ts, HEAD_DIM), lambda b, s: (s, 0)),
            _resident((RET_HEADS, rc, rc), const3),
            _resident((RET_HEADS, rc, HEAD_DIM), const3),
            _resident((RET_HEADS, F32_SUBLANES, rc), const3),
            _resident((RET_HEADS, F32_SUBLANES, HEAD_DIM), const3),
            pl.BlockSpec((None, up_rows, D_FF), lambda b, s: (layer, b * ns + s, 0)),
            pl.BlockSpec((None, dn_rows, d), lambda b, s: (layer, b * ns + s, 0)),
        ],
        out_specs=[
            pl.BlockSpec((None, ts, d), lambda b, s: (b, s, 0)),
            pl.BlockSpec((up_rows, D_FF), lambda b, s: (b * ns + s, 0)),
            pl.BlockSpec((dn_rows, d), lambda b, s: (b * ns + s, 0)),
        ],
        out_shape=[
            jax.ShapeDtypeStruct(x.shape, F32),
            jax.ShapeDtypeStruct((D_MODEL, D_FF), BF16),
            jax.ShapeDtypeStruct((D_FF, D_MODEL), BF16),
        ],
        scratch_shapes=[
            pltpu.VMEM((ts + HALO, CONV_WIDTH), F32),
            pltpu.VMEM((ts, d), BF16),
            pltpu.VMEM((RET_HEADS, HEAD_DIM, HEAD_DIM), F32),
        ],
        compiler_params=pltpu.CompilerParams(
            dimension_semantics=("arbitrary", "arbitrary"),
            vmem_limit_bytes=MIXER_VMEM_LIMIT_BYTES),
        name=f"mixer_{layer}",
    )(x, mod, norm_g, w_in, conv_w, ret_g, w_out, cos2, sin2, dec, xi, zeta, cd,
      w_up, w_down)


def _mlp_kernel(x_ref, mod_ref, g_ref, w_up_ref, w_dn_ref, fg_ref, *rest,
                layer, final, cast_next):
    o_ref = rest[2] if cast_next else rest[0]
    x = x_ref[...]
    mod = mod_ref[pl.ds(pl.program_id(0), 1), :]
    shift = mod[:, 3 * D_MODEL:4 * D_MODEL]
    scale = mod[:, 4 * D_MODEL:5 * D_MODEL]
    gate = mod[:, 5 * D_MODEL:6 * D_MODEL]
    gain = g_ref[pl.ds(layer, 1), :]
    ts = x.shape[0]
    row_chunks = [slice(i * (ts // HEAD_ROW_CHUNKS), (i + 1) * (ts // HEAD_ROW_CHUNKS))
                  for i in range(HEAD_ROW_CHUNKS)]
    hb_rows = [_modulated_rms(x[rows], gain, scale, shift).astype(BF16)
               for rows in row_chunks]
    hb = jnp.concatenate(hb_rows, axis=0)
    acc = None
    for ci in range(D_FF // FF_CHUNK):
        cols = slice(ci * FF_CHUNK, (ci + 1) * FF_CHUNK)
        if ci == 0:
            up = jnp.concatenate(
                [jnp.dot(h, w_up_ref[:, cols], preferred_element_type=F32)
                 for h in hb_rows], axis=0)
        else:
            up = jnp.dot(hb, w_up_ref[:, cols], preferred_element_type=F32)
        act = jnp.square(jnp.maximum(up, 0.0)).astype(BF16)
        if final and ci == D_FF // FF_CHUNK - 1:
            down = jnp.concatenate(
                [jnp.dot(act[rows], w_dn_ref[cols, :], preferred_element_type=F32)
                 for rows in row_chunks], axis=0)
        else:
            down = jnp.dot(act, w_dn_ref[cols, :], preferred_element_type=F32)
        acc = down if acc is None else acc + down
    y = x + gate * acc
    if final:
        r = lax.rsqrt(jnp.mean(y * y, axis=-1, keepdims=True) + EPS)
        y = (y * r) * fg_ref[...]
    o_ref[...] = y

    if cast_next:
        w_in_f32_ref, w_out_f32_ref, _, w_in_bf16_ref, w_out_bf16_ref = rest
        w_in_bf16_ref[...] = w_in_f32_ref[...].astype(BF16)
        w_out_bf16_ref[...] = w_out_f32_ref[...].astype(BF16)


def _mlp(layer, x, mod, norm_g, w_up, w_down, final_g, w_in, w_out):
    nb, t, d = x.shape
    ts = MLP_TILE
    ns = t // ts
    final = layer == DEPTH - 1
    const2 = lambda b, s: (0, 0)
    in_specs = [
        pl.BlockSpec((None, ts, d), lambda b, s: (b, s, 0)),
        _resident((None, nb, N_MOD * d), lambda b, s: (layer, 0, 0)),
        _resident((DEPTH, d), const2),
        _resident((d, D_FF), const2),
        _resident((D_FF, d), const2),
        _resident((1, d), const2),
    ]
    operands = [x, mod, norm_g, w_up, w_down, final_g]
    out_specs = [pl.BlockSpec((None, ts, d), lambda b, s: (b, s, 0))]
    out_shape = [jax.ShapeDtypeStruct(x.shape, F32)]
    if not final:
        rows = _cast_rows(D_MODEL, nb * ns)
        step_rows = lambda b, s: (b * ns + s, 0)
        next_rows = lambda b, s: (layer + 1, b * ns + s, 0)
        in_specs += [pl.BlockSpec((None, rows, IN_COLS), next_rows),
                     pl.BlockSpec((None, rows, d), next_rows)]
        operands += [w_in, w_out]
        out_specs += [pl.BlockSpec((rows, IN_COLS), step_rows),
                      pl.BlockSpec((rows, d), step_rows)]
        out_shape += [jax.ShapeDtypeStruct((D_MODEL, IN_COLS), BF16),
                      jax.ShapeDtypeStruct((D_MODEL, D_MODEL), BF16)]
    return pl.pallas_call(
        functools.partial(_mlp_kernel, layer=layer, final=final, cast_next=not final),
        grid=(nb, ns),
        in_specs=in_specs,
        out_specs=out_specs,
        out_shape=out_shape,
        compiler_params=pltpu.CompilerParams(
            dimension_semantics=("arbitrary", "arbitrary"),
            vmem_limit_bytes=VMEM_LIMIT_BYTES),
        name=f"mlp_{layer}",
    )(*operands)


def _tables(t):
    half = HEAD_DIM // 2
    inv_freq = ROPE_BASE ** (-np.arange(0, HEAD_DIM, 2, dtype=np.float64) / HEAD_DIM)
    ang = np.arange(t, dtype=np.float64)[:, None] * inv_freq[None, :]
    cos, sin = np.cos(ang), np.sin(ang)
    assert cos.shape == (t, half)
    cos2 = np.concatenate([cos, cos], axis=-1)
    sin2 = np.concatenate([-sin, sin], axis=-1)

    rc = RET_CHUNK
    k_scale = HEAD_DIM ** -0.5
    log_g = np.log1p(-np.exp2(-5.0 - np.arange(RET_HEADS, dtype=np.float64)))
    idx = np.arange(rc, dtype=np.float64)
    diff = idx[:, None] - idx[None, :]
    causal = diff >= 0
    dec = np.where(causal[None],
                   np.exp(log_g[:, None, None] * np.where(causal, diff, 0.0)[None]),
                   0.0) * k_scale
    xi = np.exp(log_g[:, None] * (idx + 1.0)[None])
    zeta = np.exp(log_g[:, None] * (rc - 1.0 - idx)[None]) * k_scale
    cd = np.exp(log_g * rc)
    lanes = (RET_HEADS, rc, HEAD_DIM)
    tables = (cos2, sin2, np.swapaxes(dec, 1, 2),
              np.broadcast_to(xi[:, :, None], lanes),
              np.broadcast_to(zeta[:, None, :], (RET_HEADS, F32_SUBLANES, rc)),
              np.broadcast_to(cd[:, None, None], (RET_HEADS, F32_SUBLANES, HEAD_DIM)))
    return tuple(jnp.asarray(a, dtype=F32) for a in tables)


def kernel(x, c, norm1_g, w_in, conv_w, ret_norm_g, w_out, norm2_g, w_up, w_down,
           w_ada, b_ada, final_g):
    nb, t, d = x.shape
    assert d == D_MODEL and t % MIX_TILE == 0 and t % MLP_TILE == 0
    assert MIX_TILE % RET_CHUNK == 0
    assert w_in.shape == (DEPTH, D_MODEL, IN_COLS)
    mod, w_in_b, w_out_b = _adaln(c, w_ada, b_ada, w_in, w_out)
    tables = _tables(t)
    fg = final_g.reshape(1, D_MODEL)
    for layer in range(DEPTH):
        x, w_up_b, w_down_b = _mixer(layer, x, mod, norm1_g, w_in_b, conv_w, ret_norm_g,
                                     w_out_b, w_up, w_down, tables)
        outs = _mlp(layer, x, mod, norm2_g, w_up_b, w_down_b, fg, w_in, w_out)
        if layer < DEPTH - 1:
            x, w_in_b, w_out_b = outs
        else:
            x, = outs
    return x
```

```python
import functools

import jax
import jax.numpy as jnp
import numpy as np
from jax import lax
from jax.experimental import pallas as pl
from jax.experimental.pallas import tpu as pltpu

D_MODEL = 1024
DEPTH = 4
CONV_WIDTH = 512
CONV_K = 3
RET_HEADS = 4
RET_WIDTH = 512
HEAD_DIM = 128
D_FF = 4 * D_MODEL
ROPE_BASE = 10000.0
EPS = 1e-6
N_MOD = 6
SEG = 512
IN_COLS = 7 * SEG

MIX_TILE = 1024
MLP_TILE = 1024
RET_CHUNK = 256
FF_CHUNK = 1024
ADA_COLS = 1536
LOOKAHEAD = 2
HEAD_ROW_CHUNKS = 4
MIX_HEAD_ROW_CHUNKS = 2
VMEM_LIMIT_BYTES = 56 * 1024 * 1024

F32 = jnp.float32
BF16 = jnp.bfloat16
F32_SUBLANES = 8
BF16_SUBLANES = 16
HALO = F32_SUBLANES


def _resident(block_shape, index_map):
    return pl.BlockSpec(block_shape, index_map, pipeline_mode=pl.Buffered(1))


def _adaln_kernel(c_ref, w_ref, b_ref, w_in_f32_ref, w_out_f32_ref,
                  o_ref, w_in_bf16_ref, w_out_bf16_ref):
    w_in_bf16_ref[...] = w_in_f32_ref[...].astype(BF16)
    w_out_bf16_ref[...] = w_out_f32_ref[...].astype(BF16)

    c = c_ref[...]
    ca = c * jax.nn.sigmoid(c)
    ca_hi = ca.astype(BF16).astype(F32)
    lhs = jnp.concatenate([ca_hi, ca - ca_hi], axis=0).astype(BF16)
    w = w_ref[...]
    w_hi = w.astype(BF16)
    w_lo = (w - w_hi.astype(F32)).astype(BF16)
    acc = (jnp.dot(lhs, w_hi, preferred_element_type=F32)
           + jnp.dot(lhs, w_lo, preferred_element_type=F32))
    nb = c.shape[0]
    bias = b_ref[pl.ds(pl.program_id(0), 1), :]
    o_ref[...] = acc[0:nb] + acc[nb:2 * nb] + bias


def _adaln(c, w_ada, b_ada, w_in, w_out):
    nb = c.shape[0]
    n_col = N_MOD * D_MODEL // ADA_COLS
    rows = _cast_rows(D_MODEL, DEPTH * n_col)
    step_rows = lambda l, j: (l * n_col + j, 0)
    first_rows = lambda l, j: (0, l * n_col + j, 0)
    return pl.pallas_call(
        _adaln_kernel,
        grid=(DEPTH, n_col),
        in_specs=[
            pl.BlockSpec((nb, D_MODEL), lambda l, j: (0, 0)),
            pl.BlockSpec((None, D_MODEL, ADA_COLS), lambda l, j: (l, 0, j)),
            pl.BlockSpec((DEPTH, ADA_COLS), lambda l, j: (0, j)),
            pl.BlockSpec((None, rows, IN_COLS), first_rows),
            pl.BlockSpec((None, rows, D_MODEL), first_rows),
        ],
        out_specs=[
            pl.BlockSpec((None, nb, ADA_COLS), lambda l, j: (l, 0, j)),
            pl.BlockSpec((rows, IN_COLS), step_rows),
            pl.BlockSpec((rows, D_MODEL), step_rows),
        ],
        out_shape=[
            jax.ShapeDtypeStruct((DEPTH, nb, N_MOD * D_MODEL), F32),
            jax.ShapeDtypeStruct((D_MODEL, IN_COLS), BF16),
            jax.ShapeDtypeStruct((D_MODEL, D_MODEL), BF16),
        ],
        compiler_params=pltpu.CompilerParams(
            dimension_semantics=("arbitrary", "arbitrary"),
            vmem_limit_bytes=VMEM_LIMIT_BYTES),
        name="adaln_mod",
    )(c, w_ada, b_ada, w_in, w_out)


def _modulated_rms(x, gain, scale, shift):
    r = lax.rsqrt(jnp.mean(x * x, axis=-1, keepdims=True) + EPS)
    return ((x * r) * gain) * (1.0 + scale) + shift


def _mixer_kernel(x_ref, mod_ref, g_ref, w_in_ref, cw_ref, rg_ref, w_out_ref,
                  cos_ref, sin_ref, dec_ref, xi_ref, zeta_ref, cd_ref,
                  w_up_f32_ref, w_dn_f32_ref,
                  o_ref, w_up_bf16_ref, w_dn_bf16_ref, ubuf, ybuf, state, *, layer):
    ts = x_ref.shape[0]

    @pl.when(pl.program_id(1) == 0)
    def _start_of_sequence():
        ubuf[0:HALO, :] = jnp.zeros((HALO, CONV_WIDTH), F32)
        state[...] = jnp.zeros(state.shape, F32)

    x = x_ref[...]
    mod = mod_ref[pl.ds(pl.program_id(0), 1), :]
    shift = mod[:, 0:D_MODEL]
    scale = mod[:, D_MODEL:2 * D_MODEL]
    gate = mod[:, 2 * D_MODEL:3 * D_MODEL]
    gain = g_ref[pl.ds(layer, 1), :]
    n_head = MIX_HEAD_ROW_CHUNKS
    hb_rows = [_modulated_rms(x[i * (ts // n_head):(i + 1) * (ts // n_head)],
                              gain, scale, shift).astype(BF16) for i in range(n_head)]
    hb = jnp.concatenate(hb_rows, axis=0)

    def proj(i):
        return jnp.dot(hb, w_in_ref[:, i * SEG:(i + 1) * SEG],
                       preferred_element_type=F32)

    q = jnp.concatenate(
        [jnp.dot(h, w_in_ref[:, 3 * SEG:4 * SEG], preferred_element_type=F32)
         for h in hb_rows], axis=0)
    k = proj(4)
    v = proj(5)

    u = proj(1) * proj(2)
    halo = ubuf[0:HALO, :]
    row = lax.broadcasted_iota(jnp.int32, (HALO, CONV_WIDTH), 0)

    def shifted(n):
        rolled = pltpu.roll(u, n, 0)
        first = jnp.where(row < n, pltpu.roll(halo, n, 0), rolled[0:HALO])
        return jnp.concatenate([first, rolled[HALO:]], axis=0)

    u1 = shifted(1)
    u2 = shifted(2)
    ubuf[HALO:HALO + ts, :] = u
    cw = cw_ref[...]
    conv = u2 * cw[0:1, :] + u1 * cw[1:2, :] + u * cw[2:3, :]
    ubuf[0:HALO, :] = ubuf[ts:ts + HALO, :]

    cos2 = cos_ref[...]
    sin2 = sin_ref[...]
    rc = dec_ref.shape[1]
    n_chunks = ts // rc
    pairs = [(hh, ci) for hh in range(RET_HEADS) for ci in range(n_chunks)]
    head_cols = [slice(hh * HEAD_DIM, (hh + 1) * HEAD_DIM) for hh in range(RET_HEADS)]
    chunk_rows = [slice(ci * rc, (ci + 1) * rc) for ci in range(n_chunks)]
    qh, kh = [], []
    for cols in head_cols:
        qc, kc = q[:, cols], k[:, cols]
        qh.append(qc * cos2 + pltpu.roll(qc, HEAD_DIM // 2, 1) * sin2)
        kh.append(kc * cos2 + pltpu.roll(kc, HEAD_DIM // 2, 1) * sin2)

    gr = proj(6)

    scores_t, kv_t, v_t = {}, {}, {}
    s_now = [state[hh] for hh in range(RET_HEADS)]
    for step in range(len(pairs) + LOOKAHEAD):
        if step < len(pairs):
            hh, ci = pairs[step]
            rows = chunk_rows[ci]
            kc = kh[hh][rows].astype(BF16)
            scores_t[hh, ci] = lax.dot_general(
                kc, qh[hh][rows].astype(BF16),
                (((1,), (1,)), ((), ())), preferred_element_type=F32)
            vt = v[rows, head_cols[hh]].T
            v_t[hh, ci] = vt.astype(BF16)
            kv_t[hh, ci] = jnp.dot((vt * zeta_ref[hh][0:1, :]).astype(BF16), kc,
                                   preferred_element_type=F32)
        if step >= LOOKAHEAD:
            hh, ci = pairs[step - LOOKAHEAD]
            rows, cols = chunk_rows[ci], head_cols[hh]
            p_t = (scores_t.pop((hh, ci)) * dec_ref[hh]).astype(BF16)
            inner_t = jnp.dot(v_t.pop((hh, ci)), p_t, preferred_element_type=F32)
            cross_t = lax.dot_general(
                s_now[hh].astype(BF16), (qh[hh][rows] * xi_ref[hh]).astype(BF16),
                (((1,), (1,)), ((), ())), preferred_element_type=F32)
            s_now[hh] = s_now[hh] * cd_ref[hh][0:1, :] + kv_t.pop((hh, ci))
            o = (inner_t + cross_t).T
            mu = jnp.mean(o, axis=-1, keepdims=True)
            oc = o - mu
            var = jnp.mean(oc * oc, axis=-1, keepdims=True)
            on = (oc * lax.rsqrt(var + EPS)) * rg_ref[pl.ds(layer, 1), cols]
            g = gr[rows, cols]
            y = (g * jax.nn.sigmoid(g)) * on
            ybuf[rows, CONV_WIDTH + hh * HEAD_DIM:
                 CONV_WIDTH + (hh + 1) * HEAD_DIM] = y.astype(BF16)
    for hh in range(RET_HEADS):
        state[hh] = s_now[hh]

    cb = proj(0)
    ybuf[:, 0:CONV_WIDTH] = (cb * conv).astype(BF16)

    mix = (jnp.dot(ybuf[:, 0:CONV_WIDTH], w_out_ref[0:CONV_WIDTH, :],
                   preferred_element_type=F32)
           + jnp.dot(ybuf[:, CONV_WIDTH:], w_out_ref[CONV_WIDTH:, :],
                     preferred_element_type=F32))
    o_ref[...] = x + gate * mix

    w_up_bf16_ref[...] = w_up_f32_ref[...].astype(BF16)
    w_dn_bf16_ref[...] = w_dn_f32_ref[...].astype(BF16)


def _cast_rows(n_rows, n_steps):
    rows = n_rows // n_steps
    assert rows * n_steps == n_rows and rows % BF16_SUBLANES == 0
    return rows


def _mixer(layer, x, mod, norm_g, w_in, conv_w, ret_g, w_out, w_up, w_down, tables):
    nb, t, d = x.shape
    cos2, sin2, dec, xi, zeta, cd = tables
    rc = dec.shape[1]
    ts = MIX_TILE
    ns = t // ts
    up_rows, dn_rows = _cast_rows(D_MODEL, nb * ns), _cast_rows(D_FF, nb * ns)
    const2 = lambda b, s: (0, 0)
    const3 = lambda b, s: (0, 0, 0)
    lay2 = lambda b, s: (layer, 0, 0)
    return pl.pallas_call(
        functools.partial(_mixer_kernel, layer=layer),
        grid=(nb, ns),
        in_specs=[
            pl.BlockSpec((None, ts, d), lambda b, s: (b, s, 0)),
            _resident((None, nb, N_MOD * d), lay2),
            _resident((DEPTH, d), const2),
            _resident((d, IN_COLS), const2),
            _resident((None, CONV_K, CONV_WIDTH), lay2),
            _resident((DEPTH, RET_WIDTH), const2),
            _resident((d, d), const2),
            pl.BlockSpec((ts, HEAD_DIM), lambda b, s: (s, 0)),
            pl.BlockSpec((ts, HEAD_DIM), lambda b, s: (s, 0)),
            _resident((RET_HEADS, rc, rc), const3),
            _resident((RET_HEADS, rc, HEAD_DIM), const3),
            _resident((RET_HEADS, F32_SUBLANES, rc), const3),
            _resident((RET_HEADS, F32_SUBLANES, HEAD_DIM), const3),
            pl.BlockSpec((None, up_rows, D_FF), lambda b, s: (layer, b * ns + s, 0)),
            pl.BlockSpec((None, dn_rows, d), lambda b, s: (layer, b * ns + s, 0)),
        ],
        out_specs=[
            pl.BlockSpec((None, ts, d), lambda b, s: (b, s, 0)),
            pl.BlockSpec((up_rows, D_FF), lambda b, s: (b * ns + s, 0)),
            pl.BlockSpec((dn_rows, d), lambda b, s: (b * ns + s, 0)),
        ],
        out_shape=[
            jax.ShapeDtypeStruct(x.shape, F32),
            jax.ShapeDtypeStruct((D_MODEL, D_FF), BF16),
            jax.ShapeDtypeStruct((D_FF, D_MODEL), BF16),
        ],
        scratch_shapes=[
            pltpu.VMEM((ts + HALO, CONV_WIDTH), F32),
            pltpu.VMEM((ts, d), BF16),
            pltpu.VMEM((RET_HEADS, HEAD_DIM, HEAD_DIM), F32),
        ],
        compiler_params=pltpu.CompilerParams(
            dimension_semantics=("arbitrary", "arbitrary"),
            vmem_limit_bytes=VMEM_LIMIT_BYTES),
        name=f"mixer_{layer}",
    )(x, mod, norm_g, w_in, conv_w, ret_g, w_out, cos2, sin2, dec, xi, zeta, cd,
      w_up, w_down)


def _mlp_kernel(x_ref, mod_ref, g_ref, w_up_ref, w_dn_ref, fg_ref, *rest,
                layer, final, cast_next):
    o_ref = rest[2] if cast_next else rest[0]
    x = x_ref[...]
    mod = mod_ref[pl.ds(pl.program_id(0), 1), :]
    shift = mod[:, 3 * D_MODEL:4 * D_MODEL]
    scale = mod[:, 4 * D_MODEL:5 * D_MODEL]
    gate = mod[:, 5 * D_MODEL:6 * D_MODEL]
    gain = g_ref[pl.ds(layer, 1), :]
    ts = x.shape[0]
    row_chunks = [slice(i * (ts // HEAD_ROW_CHUNKS), (i + 1) * (ts // HEAD_ROW_CHUNKS))
                  for i in range(HEAD_ROW_CHUNKS)]
    hb_rows = [_modulated_rms(x[rows], gain, scale, shift).astype(BF16)
               for rows in row_chunks]
    hb = jnp.concatenate(hb_rows, axis=0)
    acc = None
    for ci in range(D_FF // FF_CHUNK):
        cols = slice(ci * FF_CHUNK, (ci + 1) * FF_CHUNK)
        if ci == 0:
            up = jnp.concatenate(
                [jnp.dot(h, w_up_ref[:, cols], preferred_element_type=F32)
                 for h in hb_rows], axis=0)
        else:
            up = jnp.dot(hb, w_up_ref[:, cols], preferred_element_type=F32)
        act = jnp.square(jnp.maximum(up, 0.0)).astype(BF16)
        if final and ci == D_FF // FF_CHUNK - 1:
            down = jnp.concatenate(
                [jnp.dot(act[rows], w_dn_ref[cols, :], preferred_element_type=F32)
                 for rows in row_chunks], axis=0)
        else:
            down = jnp.dot(act, w_dn_ref[cols, :], preferred_element_type=F32)
        acc = down if acc is None else acc + down
    y = x + gate * acc
    if final:
        r = lax.rsqrt(jnp.mean(y * y, axis=-1, keepdims=True) + EPS)
        y = (y * r) * fg_ref[...]
    o_ref[...] = y

    if cast_next:
        w_in_f32_ref, w_out_f32_ref, _, w_in_bf16_ref, w_out_bf16_ref = rest
        w_in_bf16_ref[...] = w_in_f32_ref[...].astype(BF16)
        w_out_bf16_ref[...] = w_out_f32_ref[...].astype(BF16)


def _mlp(layer, x, mod, norm_g, w_up, w_down, final_g, w_in, w_out):
    nb, t, d = x.shape
    ts = MLP_TILE
    ns = t // ts
    final = layer == DEPTH - 1
    const2 = lambda b, s: (0, 0)
    in_specs = [
        pl.BlockSpec((None, ts, d), lambda b, s: (b, s, 0)),
        _resident((None, nb, N_MOD * d), lambda b, s: (layer, 0, 0)),
        _resident((DEPTH, d), const2),
        _resident((d, D_FF), const2),
        _resident((D_FF, d), const2),
        _resident((1, d), const2),
    ]
    operands = [x, mod, norm_g, w_up, w_down, final_g]
    out_specs = [pl.BlockSpec((None, ts, d), lambda b, s: (b, s, 0))]
    out_shape = [jax.ShapeDtypeStruct(x.shape, F32)]
    if not final:
        rows = _cast_rows(D_MODEL, nb * ns)
        step_rows = lambda b, s: (b * ns + s, 0)
        next_rows = lambda b, s: (layer + 1, b * ns + s, 0)
        in_specs += [pl.BlockSpec((None, rows, IN_COLS), next_rows),
                     pl.BlockSpec((None, rows, d), next_rows)]
        operands += [w_in, w_out]
        out_specs += [pl.BlockSpec((rows, IN_COLS), step_rows),
                      pl.BlockSpec((rows, d), step_rows)]
        out_shape += [jax.ShapeDtypeStruct((D_MODEL, IN_COLS), BF16),
                      jax.ShapeDtypeStruct((D_MODEL, D_MODEL), BF16)]
    return pl.pallas_call(
        functools.partial(_mlp_kernel, layer=layer, final=final, cast_next=not final),
        grid=(nb, ns),
        in_specs=in_specs,
        out_specs=out_specs,
        out_shape=out_shape,
        compiler_params=pltpu.CompilerParams(
            dimension_semantics=("arbitrary", "arbitrary"),
            vmem_limit_bytes=VMEM_LIMIT_BYTES),
        name=f"mlp_{layer}",
    )(*operands)


def _tables(t):
    half = HEAD_DIM // 2
    inv_freq = ROPE_BASE ** (-np.arange(0, HEAD_DIM, 2, dtype=np.float64) / HEAD_DIM)
    ang = np.arange(t, dtype=np.float64)[:, None] * inv_freq[None, :]
    cos, sin = np.cos(ang), np.sin(ang)
    assert cos.shape == (t, half)
    cos2 = np.concatenate([cos, cos], axis=-1)
    sin2 = np.concatenate([-sin, sin], axis=-1)

    rc = RET_CHUNK
    k_scale = HEAD_DIM ** -0.5
    log_g = np.log1p(-np.exp2(-5.0 - np.arange(RET_HEADS, dtype=np.float64)))
    idx = np.arange(rc, dtype=np.float64)
    diff = idx[:, None] - idx[None, :]
    causal = diff >= 0
    dec = np.where(causal[None],
                   np.exp(log_g[:, None, None] * np.where(causal, diff, 0.0)[None]),
                   0.0) * k_scale
    xi = np.exp(log_g[:, None] * (idx + 1.0)[None])
    zeta = np.exp(log_g[:, None] * (rc - 1.0 - idx)[None]) * k_scale
    cd = np.exp(log_g * rc)
    lanes = (RET_HEADS, rc, HEAD_DIM)
    tables = (cos2, sin2, np.swapaxes(dec, 1, 2),
              np.broadcast_to(xi[:, :, None], lanes),
              np.broadcast_to(zeta[:, None, :], (RET_HEADS, F32_SUBLANES, rc)),
              np.broadcast_to(cd[:, None, None], (RET_HEADS, F32_SUBLANES, HEAD_DIM)))
    return tuple(jnp.asarray(a, dtype=F32) for a in tables)


def kernel(x, c, norm1_g, w_in, conv_w, ret_norm_g, w_out, norm2_g, w_up, w_down,
           w_ada, b_ada, final_g):
    nb, t, d = x.shape
    assert d == D_MODEL and t % MIX_TILE == 0 and t % MLP_TILE == 0
    assert MIX_TILE % RET_CHUNK == 0
    assert w_in.shape == (DEPTH, D_MODEL, IN_COLS)
    mod, w_in_b, w_out_b = _adaln(c, w_ada, b_ada, w_in, w_out)
    tables = _tables(t)
    fg = final_g.reshape(1, D_MODEL)
    for layer in range(DEPTH):
        x, w_up_b, w_down_b = _mixer(layer, x, mod, norm1_g, w_in_b, conv_w, ret_norm_g,
                                     w_out_b, w_up, w_down, tables)
        outs = _mlp(layer, x, mod, norm2_g, w_up_b, w_down_b, fg, w_in, w_out)
        if layer < DEPTH - 1:
            x, w_in_b, w_out_b = outs
        else:
            x, = outs
    return x
```
